```python
import math
import jax, jax.numpy as jnp
from jax import lax
import numpy as np

D_MODEL = 1024
BATCH = 8
SEQ = 2048
DEPTH = 4

CHUNK = 64
N_HEADS = 8
N_KV_HEADS = 2
HEAD_DIM = 64
ATTN_WIDTH = N_HEADS * HEAD_DIM
ROPE_DIM = HEAD_DIM // 4
ROPE_THETA = 500000.0
N_IDX_HEADS = 4
IDX_DIM = 64
TOPK_MAX = 256
Q_BLOCK = 128
SGU_CHUNK = 128
SGU_GROUPS = 8
SGU_WIDTH = D_MODEL - ATTN_WIDTH
SGU_GROUP_DIM = SGU_WIDTH // SGU_GROUPS
D_FF = -(-(8 * D_MODEL) // (3 * 256)) * 256
PLE_DIM = 256
ALPHA = (2 * DEPTH) ** 0.25
BETA = (8 * DEPTH) ** -0.25
LN_EPS = 1e-5

Q_COLS = N_HEADS * HEAD_DIM
KV_COLS = N_KV_HEADS * HEAD_DIM
IQ_COLS = N_IDX_HEADS * IDX_DIM
IK_COLS = IDX_DIM
IW_COLS = N_IDX_HEADS
IN_SPLITS = (Q_COLS, KV_COLS, KV_COLS, IQ_COLS, IK_COLS, IW_COLS, SGU_WIDTH, SGU_WIDTH)
IN_COLS = Q_COLS + 2 * KV_COLS + IQ_COLS + IK_COLS + IW_COLS + 2 * SGU_WIDTH

kernel_name = "hybrid_dsa_gmlp_deepnorm_encoder"


def layer_norm(x, g, b):
    xf = x.astype(jnp.float32)
    mu = jnp.mean(xf, axis=-1, keepdims=True)
    var = jnp.mean(jnp.square(xf - mu), axis=-1, keepdims=True)
    y = (xf - mu) * lax.rsqrt(var + LN_EPS)
    return (y * g.astype(jnp.float32) + b.astype(jnp.float32)).astype(x.dtype)


def partial_rope(x, pos):
    half = ROPE_DIM // 2
    inv = ROPE_THETA ** (-2.0 * jnp.arange(half, dtype=jnp.float32) / ROPE_DIM)
    ang = pos.astype(jnp.float32)[..., None] * inv
    cos = jnp.cos(ang)[:, :, None, :]
    sin = jnp.sin(ang)[:, :, None, :]
    xr = x[..., :ROPE_DIM].astype(jnp.float32)
    x1, x2 = xr[..., :half], xr[..., half:]
    rot = jnp.concatenate([x1 * cos - x2 * sin, x2 * cos + x1 * sin], axis=-1)
    return jnp.concatenate([rot.astype(x.dtype), x[..., ROPE_DIM:]], axis=-1)


def dsa_attention(q, k, v, qi, ki, wi):
    B, S = q.shape[0], q.shape[1]
    topk = min(TOPK_MAX, S // 4)
    nb = S // Q_BLOCK
    G = N_KV_HEADS
    R = N_HEADS // N_KV_HEADS
    key_chunk = jnp.arange(S) // CHUNK

    def to_blocks(a):
        return jnp.moveaxis(a.reshape((B, nb, Q_BLOCK) + a.shape[2:]), 1, 0)

    q_b = to_blocks(q.reshape(B, S, G, R, HEAD_DIM))
    qi_b = to_blocks(qi)
    wi_b = to_blocks(wi)
    t_b = jnp.arange(S).reshape(nb, Q_BLOCK)

    def block(args):
        qb, qib, wib, tb = args
        q_chunk = tb // CHUNK
        allowed = key_chunk[None, :] <= q_chunk[:, None]
        logits = jnp.einsum('bqhd,bsd->bqhs', qib, ki)
        score = jnp.einsum('bqhs,bqh->bqs', jax.nn.relu(logits), wib).astype(jnp.float32)
        score = jnp.where(allowed[None], score, -jnp.inf)
        _, idx = lax.top_k(score, topk)
        k_sel = jax.vmap(lambda kb, ib: kb[ib])(k, idx)
        v_sel = jax.vmap(lambda vb, ib: vb[ib])(v, idx)
        valid = key_chunk[idx] <= q_chunk[None, :, None]
        s = jnp.einsum('bqgrd,bqkgd->bqgrk', qb, k_sel).astype(jnp.float32) * (HEAD_DIM ** -0.5)
        s = jnp.where(valid[:, :, None, None, :], s, -jnp.inf)
        pr = jax.nn.softmax(s, axis=-1).astype(v.dtype)
        return jnp.einsum('bqgrk,bqkgd->bqgrd', pr, v_sel)

    out = lax.map(block, (q_b, qi_b, wi_b, t_b))
    return jnp.moveaxis(out, 0, 1).reshape(B, S, ATTN_WIDTH)


def spatial_gating(u, v, w_s, b_s, g, bta):
    B, S = u.shape[0], u.shape[1]
    v = layer_norm(v, g, bta)
    nc = S // SGU_CHUNK
    vb = v.reshape(B, nc, SGU_CHUNK, SGU_GROUPS, SGU_GROUP_DIM)
    tri = jnp.tril(jnp.ones((SGU_CHUNK, SGU_CHUNK), dtype=bool))
    w = jnp.where(tri[None], w_s, jnp.zeros_like(w_s))
    mixed = jnp.einsum('gts,bcsgd->bctgd', w, vb) + jnp.swapaxes(b_s, 0, 1)[None, None, :, :, None]
    return u * mixed.reshape(B, S, SGU_WIDTH)


def setup_inputs(seed: int = 0) -> dict:
    key = jax.random.key(seed)
    ks = jax.random.split(key, 20)

    def nrm(k, shape, scale):
        return jax.random.normal(k, shape, jnp.float32) * scale

    x = nrm(ks[0], (BATCH, SEQ, D_MODEL), 1.0)
    p = nrm(ks[1], (DEPTH, BATCH, SEQ, PLE_DIM), 1.0)
    start = jax.random.randint(ks[2], (BATCH, 1), 0, 64, dtype=jnp.int32) * CHUNK
    positions = (start + jnp.arange(SEQ, dtype=jnp.int32)[None, :]).astype(jnp.int32)
    return {
        "x": x,
        "p": p,
        "positions": positions,
        "w_in": nrm(ks[3], (DEPTH, D_MODEL, IN_COLS), D_MODEL ** -0.5),
        "w_o": nrm(ks[4], (DEPTH, D_MODEL, D_MODEL), BETA * D_MODEL ** -0.5),
        "ln1_g": 1.0 + nrm(ks[5], (DEPTH, D_MODEL), 0.02),
        "ln1_b": nrm(ks[6], (DEPTH, D_MODEL), 0.02),
        "ln2_g": 1.0 + nrm(ks[7], (DEPTH, D_MODEL), 0.02),
        "ln2_b": nrm(ks[8], (DEPTH, D_MODEL), 0.02),
        "sgu_w": nrm(ks[9], (DEPTH, SGU_GROUPS, SGU_CHUNK, SGU_CHUNK), SGU_CHUNK ** -0.5),
        "sgu_b": 1.0 + nrm(ks[10], (DEPTH, SGU_GROUPS, SGU_CHUNK), 0.02),
        "sgu_ln_g": 1.0 + nrm(ks[11], (DEPTH, SGU_WIDTH), 0.02),
        "sgu_ln_b": nrm(ks[12], (DEPTH, SGU_WIDTH), 0.02),
        "w_ffn_in": nrm(ks[13], (DEPTH, D_MODEL, 2 * D_FF), D_MODEL ** -0.5),
        "w_ffn_out": nrm(ks[14], (DEPTH, D_FF, D_MODEL), BETA * D_FF ** -0.5),
        "w_ple": nrm(ks[15], (DEPTH, PLE_DIM, D_MODEL), PLE_DIM ** -0.5),
        "w_ple_gate": nrm(ks[16], (DEPTH, D_MODEL, D_MODEL), D_MODEL ** -0.5),
    }


def reference(x, p, positions, w_in, w_o, ln1_g, ln1_b, ln2_g, ln2_b, sgu_w, sgu_b,
              sgu_ln_g, sgu_ln_b, w_ffn_in, w_ffn_out, w_ple, w_ple_gate):
    B, S = x.shape[0], x.shape[1]
    split_points = [int(o) for o in np.cumsum(IN_SPLITS)[:-1]]
    for i in range(DEPTH):
        h = x @ w_in[i]
        q, k, v, qi, ki, wi, u, vs = jnp.split(h, split_points, axis=-1)
        q = partial_rope(q.reshape(B, S, N_HEADS, HEAD_DIM), positions)
        k = partial_rope(k.reshape(B, S, N_KV_HEADS, HEAD_DIM), positions)
        v = v.reshape(B, S, N_KV_HEADS, HEAD_DIM)
        qi = partial_rope(qi.reshape(B, S, N_IDX_HEADS, IDX_DIM), positions)
        ki = partial_rope(ki.reshape(B, S, 1, IDX_DIM), positions)[:, :, 0]
        wi = wi * ((N_IDX_HEADS * IDX_DIM) ** -0.5)
        a_out = dsa_attention(q, k, v, qi, ki, wi)
        b_out = spatial_gating(jax.nn.gelu(u), jax.nn.gelu(vs), sgu_w[i], sgu_b[i],
                               sgu_ln_g[i], sgu_ln_b[i])
        mix = jnp.concatenate([a_out, b_out], axis=-1) @ w_o[i]
        x = layer_norm(ALPHA * x + mix, ln1_g[i], ln1_b[i])
        gate, up = jnp.split(x @ w_ffn_in[i], 2, axis=-1)
        ffn = (jax.nn.silu(gate) * up) @ w_ffn_out[i]
        ple = (p[i] @ w_ple[i]) * jax.nn.sigmoid(x @ w_ple_gate[i])
        x = layer_norm(ALPHA * x + ffn + ple, ln2_g[i], ln2_b[i])
    return x
```

```python
import functools
import math

import jax
import jax.numpy as jnp
from jax import lax
from jax.experimental import pallas as pl
from jax.experimental.pallas import tpu as pltpu

D_MODEL = 1024
CHUNK = 64
N_HEADS = 8
N_KV_HEADS = 2
HEAD_DIM = 64
ATTN_WIDTH = N_HEADS * HEAD_DIM
ROPE_DIM = HEAD_DIM // 4
ROPE_HALF = ROPE_DIM // 2
ROPE_THETA = 500000.0
N_IDX_HEADS = 4
IDX_DIM = 64
TOPK_MAX = 256
Q_BLOCK = 128
SGU_CHUNK = 128
SGU_GROUPS = 8
SGU_WIDTH = D_MODEL - ATTN_WIDTH
SGU_GROUP_DIM = SGU_WIDTH // SGU_GROUPS
D_FF = 2816
PLE_DIM = 256
LN_EPS = 1e-5

Q_COLS = N_HEADS * HEAD_DIM
KV_COLS = N_KV_HEADS * HEAD_DIM
IQ_COLS = N_IDX_HEADS * IDX_DIM

ROW_Q = 0
ROW_K = ROW_Q + Q_COLS
ROW_V = ROW_K + KV_COLS
ROW_QI = ROW_V + KV_COLS
ROW_KI = ROW_QI + IQ_COLS
ROW_WI = ROW_KI + IDX_DIM
ATTN_ROWS = 1152

LANES = 128
SUBLANES = 8
VMEM_LIMIT = 56 * 1024 * 1024

PROJ_TM = 512
MIX_TM = 512
FFN_TM = 512
FFN_CHUNK = 256
KEY_BITS = 32
NEG_INF = float("-inf")


def _layer_norm(y, g, b):
    mu = jnp.mean(y, axis=-1, keepdims=True)
    d = y - mu
    var = jnp.mean(d * d, axis=-1, keepdims=True)
    return d * lax.rsqrt(var + LN_EPS) * g + b


def _dot(a, b):
    return jnp.dot(a, b, preferred_element_type=jnp.float32)


def _dot_nt(a, b):
    return lax.dot_general(a, b, (((1,), (1,)), ((), ())),
                           preferred_element_type=jnp.float32)


def _rope_table_kernel(pos_ref, cos_ref, sin_ref):
    seq = pos_ref.shape[-1]
    pos = pos_ref[0].astype(jnp.float32)
    i = lax.broadcasted_iota(jnp.int32, (ROPE_HALF, seq), 0).astype(jnp.float32)
    inv = jnp.power(jnp.float32(ROPE_THETA), i * (-2.0 / ROPE_DIM))
    ang = pos * inv
    cos_ref[0] = jnp.cos(ang)
    sin_ref[0] = jnp.sin(ang)


def _rope_tables(positions):
    b, s = positions.shape
    pos3 = positions.reshape(b, 1, s)
    out = jax.ShapeDtypeStruct((b, ROPE_HALF, s), jnp.float32)
    return pl.pallas_call(
        _rope_table_kernel,
        grid=(b,),
        in_specs=[pl.BlockSpec((1, 1, s), lambda i: (i, 0, 0))],
        out_specs=[pl.BlockSpec((1, ROPE_HALF, s), lambda i: (i, 0, 0))] * 2,
        out_shape=[out, out],
        name="rope_tables",
    )(pos3)


def _rope_rows(ht, cos, sin):
    x1 = ht[0:ROPE_HALF]
    x2 = ht[ROPE_HALF:ROPE_DIM]
    return jnp.concatenate(
        [x1 * cos - x2 * sin, x2 * cos + x1 * sin, ht[ROPE_DIM:]], axis=0)


def _proj_kernel(x_ref, wta_ref, wsgu_ref, cos_ref, sin_ref, sw_ref, sb_ref,
                 lng_ref, lnb_ref,
                 qt_ref, kn_ref, vt_ref, qit_ref, kin_ref, wit_ref, bo_ref):
    tm = x_ref.shape[1]
    xb = x_ref[0].astype(jnp.bfloat16)
    cos = cos_ref[0]
    sin = sin_ref[0]

    qt = _dot_nt(wta_ref[ROW_Q:ROW_K, :], xb)
    for h in range(N_HEADS):
        head = _rope_rows(qt[h * HEAD_DIM:(h + 1) * HEAD_DIM], cos, sin)
        qt_ref[0, h * HEAD_DIM:(h + 1) * HEAD_DIM, :] = (
            head * (HEAD_DIM ** -0.5)).astype(qt_ref.dtype)

    kt = _dot_nt(wta_ref[ROW_K:ROW_V, :], xb)
    kt = jnp.concatenate(
        [_rope_rows(kt[g * HEAD_DIM:(g + 1) * HEAD_DIM], cos, sin)
         for g in range(N_KV_HEADS)], axis=0)
    kn = kt.T

    vt = _dot_nt(wta_ref[ROW_V:ROW_QI, :], xb)

    qit = _dot_nt(wta_ref[ROW_QI:ROW_KI, :], xb)
    for h in range(N_IDX_HEADS):
        head = _rope_rows(qit[h * IDX_DIM:(h + 1) * IDX_DIM], cos, sin)
        qit_ref[0, h * IDX_DIM:(h + 1) * IDX_DIM, :] = head.astype(qit_ref.dtype)

    kwt = _dot_nt(wta_ref[ROW_KI:ATTN_ROWS, :], xb)
    kit = _rope_rows(kwt[0:IDX_DIM], cos, sin)
    kit = jnp.concatenate([kit, jnp.zeros_like(kit)], axis=0)
    kin = kit.T
    wit_ref[0] = kwt[IDX_DIM:IDX_DIM + SUBLANES] * ((N_IDX_HEADS * IDX_DIM) ** -0.5)

    for c in range(tm // Q_BLOCK):
        rows = slice(c * Q_BLOCK, (c + 1) * Q_BLOCK)
        kn_ref[0, c] = kn[rows].astype(kn_ref.dtype)
        kin_ref[0, c] = kin[rows].astype(kin_ref.dtype)
        vt_ref[0, c] = vt[:, rows].astype(vt_ref.dtype)

    uv = _dot(xb, wsgu_ref[...])
    gu = jax.nn.gelu(uv[:, :SGU_WIDTH])
    gv = jax.nn.gelu(uv[:, SGU_WIDTH:])
    gv = _layer_norm(gv, lng_ref[...], lnb_ref[...]).astype(jnp.bfloat16)

    row = lax.broadcasted_iota(jnp.int32, (SGU_CHUNK, SGU_CHUNK), 0)
    col = lax.broadcasted_iota(jnp.int32, (SGU_CHUNK, SGU_CHUNK), 1)
    tri = col <= row
    ws = [jnp.where(tri, sw_ref[g], 0.0).astype(jnp.bfloat16)
          for g in range(SGU_GROUPS)]
    lane = lax.broadcasted_iota(jnp.int32, (SGU_CHUNK, LANES), 1)
    first_group = lane < SGU_GROUP_DIM
    for c in range(tm // SGU_CHUNK):
        rows = slice(c * SGU_CHUNK, (c + 1) * SGU_CHUNK)
        for pr in range(SGU_WIDTH // LANES):
            cols = slice(pr * LANES, (pr + 1) * LANES)
            vblk = gv[rows, cols]
            mixed = jnp.where(first_group,
                              _dot(ws[2 * pr], vblk), _dot(ws[2 * pr + 1], vblk))
            mixed = mixed + sb_ref[:, cols]
            bo_ref[0, rows, cols] = (gu[rows, cols] * mixed).astype(bo_ref.dtype)


def _proj(x, wta, wsgu, cos_t, sin_t, sgu_w, sgu_bias, ln_g, ln_b):
    b, s, d = x.shape
    tm = PROJ_TM
    nb = s // Q_BLOCK
    bpt = tm // Q_BLOCK
    grid = (b, s // tm)
    const2 = lambda i, j: (0, 0)
    in_specs = [
        pl.BlockSpec((1, tm, d), lambda i, j: (i, j, 0)),
        pl.BlockSpec(wta.shape, const2),
        pl.BlockSpec(wsgu.shape, const2),
        pl.BlockSpec((1, ROPE_HALF, tm), lambda i, j: (i, 0, j)),
        pl.BlockSpec((1, ROPE_HALF, tm), lambda i, j: (i, 0, j)),
        pl.BlockSpec(sgu_w.shape, lambda i, j: (0, 0, 0)),
        pl.BlockSpec(sgu_bias.shape, const2),
        pl.BlockSpec(ln_g.shape, const2),
        pl.BlockSpec(ln_b.shape, const2),
    ]
    blk4 = pl.BlockSpec((1, bpt, Q_BLOCK, LANES), lambda i, j: (i, j, 0, 0))
    out_specs = [
        pl.BlockSpec((1, Q_COLS, tm), lambda i, j: (i, 0, j)),
        blk4,
        blk4,
        pl.BlockSpec((1, IQ_COLS, tm), lambda i, j: (i, 0, j)),
        blk4,
        pl.BlockSpec((1, SUBLANES, tm), lambda i, j: (i, 0, j)),
        pl.BlockSpec((1, tm, SGU_WIDTH), lambda i, j: (i, j, 0)),
    ]
    bf = jnp.bfloat16
    out_shape = [
        jax.ShapeDtypeStruct((b, Q_COLS, s), bf),
        jax.ShapeDtypeStruct((b, nb, Q_BLOCK, LANES), bf),
        jax.ShapeDtypeStruct((b, nb, LANES, Q_BLOCK), bf),
        jax.ShapeDtypeStruct((b, IQ_COLS, s), bf),
        jax.ShapeDtypeStruct((b, nb, Q_BLOCK, LANES), bf),
        jax.ShapeDtypeStruct((b, SUBLANES, s), jnp.float32),
        jax.ShapeDtypeStruct((b, s, SGU_WIDTH), bf),
    ]
    return pl.pallas_call(
        _proj_kernel,
        grid=grid,
        in_specs=in_specs,
        out_specs=out_specs,
        out_shape=out_shape,
        compiler_params=pltpu.CompilerParams(
            dimension_semantics=("parallel", "parallel"),
            vmem_limit_bytes=VMEM_LIMIT),
        name="proj",
    )(x, wta, wsgu, cos_t, sin_t, sgu_w, sgu_bias, ln_g, ln_b)


def _sublane_total(acc):
    return jnp.broadcast_to(jnp.sum(acc, axis=0, keepdims=True), acc.shape)


def _key_to_float(key):
    bits = key ^ ((key >> 31) & jnp.int32(0x7FFFFFFF))
    return pltpu.bitcast(bits, jnp.float32)


KEY_OF_NEG_INF = -0x7F800001


def _attn_kernel(qit_ref, wit_ref, qt_ref, kin_ref, kn_ref, vt_ref, at_ref,
                 score_ref, bias_ref, s_ref):
    j = pl.program_id(1)
    nk = j + 1
    topk = TOPK_MAX
    vshape = (SUBLANES, Q_BLOCK)
    sub = Q_BLOCK // SUBLANES

    zeros_i = jnp.zeros((LANES - IDX_DIM, N_IDX_HEADS * Q_BLOCK), jnp.bfloat16)
    qi_cat = jnp.concatenate(
        [qit_ref[0, h * IDX_DIM:(h + 1) * IDX_DIM, :] for h in range(N_IDX_HEADS)],
        axis=1)
    qi_cat = jnp.concatenate([qi_cat, zeros_i], axis=0)
    wit = wit_ref[0]
    krow = lax.broadcasted_iota(jnp.int32, (Q_BLOCK, Q_BLOCK), 0)
    qcol = lax.broadcasted_iota(jnp.int32, (Q_BLOCK, Q_BLOCK), 1)
    diag_hidden = (krow >= CHUNK) & (qcol < CHUNK)

    def score_body(c, carry):
        logits = _dot(kin_ref[0, c], qi_cat)
        sc = jnp.zeros((Q_BLOCK, Q_BLOCK), jnp.float32)
        for h in range(N_IDX_HEADS):
            sc = sc + jnp.maximum(logits[:, h * Q_BLOCK:(h + 1) * Q_BLOCK], 0.0) * wit[h:h + 1, :]
        sc = jnp.where(jnp.logical_and(c == j, diag_hidden), NEG_INF, sc)
        score_ref[c] = sc
        return carry

    lax.fori_loop(0, nk, score_body, 0)

    @pl.when(nk * Q_BLOCK <= topk)
    def _():
        def body(c, carry):
            sc = score_ref[c]
            bias_ref[c] = jnp.where(sc == NEG_INF, NEG_INF, 0.0)
            return carry
        lax.fori_loop(0, nk, body, 0)

    @pl.when(nk * Q_BLOCK > topk)
    def _():
        def count_ge(cand):
            def body(c, acc):
                blk = score_ref[c].reshape(sub, SUBLANES, Q_BLOCK)
                return acc + jnp.sum(jnp.where(blk >= cand[None], 1.0, 0.0), axis=0)
            return _sublane_total(lax.fori_loop(0, nk, body, jnp.zeros(vshape, jnp.float32)))

        def bit_body(i, ukey):
            bit = jnp.left_shift(jnp.int32(1), (KEY_BITS - 1) - i)
            trial = ukey | bit
            key = jnp.maximum(trial ^ jnp.int32(-0x80000000), KEY_OF_NEG_INF)
            cnt = count_ge(_key_to_float(key))
            return jnp.where(cnt >= topk, trial, ukey)

        ukey = lax.fori_loop(0, KEY_BITS, bit_body, jnp.zeros(vshape, jnp.int32))
        thr = _key_to_float(jnp.maximum(ukey ^ jnp.int32(-0x80000000), KEY_OF_NEG_INF))

        def gt_body(c, acc):
            blk = score_ref[c].reshape(sub, SUBLANES, Q_BLOCK)
            return acc + jnp.sum(jnp.where(blk > thr[None], 1.0, 0.0), axis=0)
        n_gt = _sublane_total(lax.fori_loop(0, nk, gt_body, jnp.zeros(vshape, jnp.float32)))
        room = (topk - n_gt)[0:1, :]
        thr_row = thr[0:1, :]

        incl = (qcol <= krow).astype(jnp.bfloat16)

        def tie_body(c, seen):
            sc = score_ref[c]
            tie = sc == thr_row
            rank = _dot(incl, jnp.where(tie, 1.0, 0.0).astype(jnp.bfloat16)) + seen
            keep_tie = jnp.where(tie, jnp.where(rank <= room, 0.0, NEG_INF), NEG_INF)
            bias = jnp.where(sc > thr_row, 0.0, keep_tie)
            bias_ref[c] = jnp.where(sc == NEG_INF, NEG_INF, bias)
            return rank[Q_BLOCK - 1:Q_BLOCK, :]

        lax.fori_loop(0, nk, tie_body, jnp.zeros((1, Q_BLOCK), jnp.float32))

    rep = N_HEADS // N_KV_HEADS
    zeros_q = jnp.zeros((HEAD_DIM, rep * Q_BLOCK), jnp.bfloat16)
    for g in range(N_KV_HEADS):
        q_cat = jnp.concatenate(
            [qt_ref[0, (g * rep + r) * HEAD_DIM:(g * rep + r + 1) * HEAD_DIM, :]
             for r in range(rep)], axis=1)
        q_pad = jnp.concatenate([q_cat, zeros_q] if g == 0 else [zeros_q, q_cat], axis=0)
        mshape = (SUBLANES, rep * Q_BLOCK)

        def qk_body(c, m_acc):
            s = _dot(kn_ref[0, c], q_pad)
            bias = bias_ref[c]
            s = s + jnp.concatenate([bias] * rep, axis=1)
            s_ref[c] = s
            return jnp.maximum(m_acc, jnp.max(s.reshape(sub, SUBLANES, rep * Q_BLOCK), axis=0))

        m_acc = lax.fori_loop(0, nk, qk_body, jnp.full(mshape, NEG_INF, jnp.float32))
        m_row = jnp.max(m_acc, axis=0, keepdims=True)

        def pv_body(c, carry):
            l_acc, o_acc = carry
            p = jnp.exp(s_ref[c] - m_row)
            l_acc = l_acc + jnp.sum(p.reshape(sub, SUBLANES, rep * Q_BLOCK), axis=0)
            v_blk = vt_ref[0, c, g * HEAD_DIM:(g + 1) * HEAD_DIM, :]
            o_acc = o_acc + _dot(v_blk, p.astype(jnp.bfloat16))
            return l_acc, o_acc

        l_acc, o_acc = lax.fori_loop(
            0, nk, pv_body,
            (jnp.zeros(mshape, jnp.float32),
             jnp.zeros((HEAD_DIM, rep * Q_BLOCK), jnp.float32)))
        l_row = jnp.sum(l_acc, axis=0, keepdims=True)
        out = o_acc / l_row
        for r in range(rep):
            h = g * rep + r
            at_ref[0, h * HEAD_DIM:(h + 1) * HEAD_DIM, :] = (
                out[:, r * Q_BLOCK:(r + 1) * Q_BLOCK].astype(at_ref.dtype))


def _attn(qit, wit, qt, kin, kn, vt):
    b, _, s = qt.shape
    nb = s // Q_BLOCK
    per_q = lambda rows: pl.BlockSpec((1, rows, Q_BLOCK), lambda i, j: (i, 0, j))
    per_b = pl.BlockSpec((1, nb, Q_BLOCK, LANES), lambda i, j: (i, 0, 0, 0))
    return pl.pallas_call(
        _attn_kernel,
        grid=(b, nb),
        in_specs=[per_q(IQ_COLS), per_q(SUBLANES), per_q(Q_COLS), per_b, per_b, per_b],
        out_specs=per_q(ATTN_WIDTH),
        out_shape=jax.ShapeDtypeStruct((b, ATTN_WIDTH, s), jnp.bfloat16),
        scratch_shapes=[
            pltpu.VMEM((nb, Q_BLOCK, Q_BLOCK), jnp.float32),
            pltpu.VMEM((nb, Q_BLOCK, Q_BLOCK), jnp.float32),
            pltpu.VMEM((nb, Q_BLOCK, (N_HEADS // N_KV_HEADS) * Q_BLOCK), jnp.float32),
        ],
        compiler_params=pltpu.CompilerParams(
            dimension_semantics=("parallel", "arbitrary"),
            vmem_limit_bytes=VMEM_LIMIT),
        name="attn",
    )(qit, wit, qt, kin, kn, vt)


def _mix_kernel(alpha, x_ref, at_ref, bo_ref, wo_ref, g_ref, b_ref, o_ref):
    a_part = lax.dot_general(at_ref[0], wo_ref[0:ATTN_WIDTH, :],
                             (((0,), (0,)), ((), ())),
                             preferred_element_type=jnp.float32)
    b_part = _dot(bo_ref[0], wo_ref[ATTN_WIDTH:, :])
    y = alpha * x_ref[0] + a_part + b_part
    o_ref[0] = _layer_norm(y, g_ref[...], b_ref[...])


def _mix(x, at, bo, wo, g, bta, alpha):
    b, s, d = x.shape
    tm = MIX_TM
    const2 = lambda i, j: (0, 0)
    return pl.pallas_call(
        functools.partial(_mix_kernel, alpha),
        grid=(b, s // tm),
        in_specs=[
            pl.BlockSpec((1, tm, d), lambda i, j: (i, j, 0)),
            pl.BlockSpec((1, ATTN_WIDTH, tm), lambda i, j: (i, 0, j)),
            pl.BlockSpec((1, tm, SGU_WIDTH), lambda i, j: (i, j, 0)),
            pl.BlockSpec(wo.shape, const2),
            pl.BlockSpec(g.shape, const2),
            pl.BlockSpec(bta.shape, const2),
        ],
        out_specs=pl.BlockSpec((1, tm, d), lambda i, j: (i, j, 0)),
        out_shape=jax.ShapeDtypeStruct(x.shape, x.dtype),
        compiler_params=pltpu.CompilerParams(
            dimension_semantics=("parallel", "parallel"),
            vmem_limit_bytes=VMEM_LIMIT),
        name="mix",
    )(x, at, bo, wo, g, bta)


def _ffn_kernel(alpha, x_ref, p_ref, win_ref, wout_ref, wple_ref, wgate_ref,
                g_ref, b_ref, o_ref, acc_ref):
    x = x_ref[...]
    xb = x.astype(jnp.bfloat16)
    gate_lin = _dot(xb, wgate_ref[...])
    ple = _dot(p_ref[...].astype(jnp.bfloat16), wple_ref[...]) * jax.nn.sigmoid(gate_lin)
    acc_ref[...] = alpha * x + ple
    for c in range(D_FF // FFN_CHUNK):
        cols = slice(c * FFN_CHUNK, (c + 1) * FFN_CHUNK)
        up_cols = slice(D_FF + c * FFN_CHUNK, D_FF + (c + 1) * FFN_CHUNK)
        gate = _dot(xb, win_ref[:, cols])
        up = _dot(xb, win_ref[:, up_cols])
        h = (jax.nn.silu(gate) * up).astype(jnp.bfloat16)
        acc_ref[...] += _dot(h, wout_ref[cols, :])
    o_ref[...] = _layer_norm(acc_ref[...], g_ref[...], b_ref[...])


def _ffn(x2, p2, win, wout, wple, wgate, g, bta, alpha):
    n, d = x2.shape
    tm = FFN_TM
    const2 = lambda i: (0, 0)
    return pl.pallas_call(
        functools.partial(_ffn_kernel, alpha),
        grid=(n // tm,),
        in_specs=[
            pl.BlockSpec((tm, d), lambda i: (i, 0)),
            pl.BlockSpec((tm, PLE_DIM), lambda i: (i, 0)),
            pl.BlockSpec(win.shape, const2, pipeline_mode=pl.Buffered(1)),
            pl.BlockSpec(wout.shape, const2, pipeline_mode=pl.Buffered(1)),
            pl.BlockSpec(wple.shape, const2, pipeline_mode=pl.Buffered(1)),
            pl.BlockSpec(wgate.shape, const2, pipeline_mode=pl.Buffered(1)),
            pl.BlockSpec(g.shape, const2),
            pl.BlockSpec(bta.shape, const2),
        ],
        out_specs=pl.BlockSpec((tm, d), lambda i: (i, 0)),
        out_shape=jax.ShapeDtypeStruct(x2.shape, x2.dtype),
        scratch_shapes=[pltpu.VMEM((tm, d), jnp.float32)],
        compiler_params=pltpu.CompilerParams(
            dimension_semantics=("parallel",),
            vmem_limit_bytes=VMEM_LIMIT),
        name="ffn",
    )(x2, p2, win, wout, wple, wgate, g, bta)


def kernel(x, p, positions, w_in, w_o, ln1_g, ln1_b, ln2_g, ln2_b, sgu_w, sgu_b,
           sgu_ln_g, sgu_ln_b, w_ffn_in, w_ffn_out, w_ple, w_ple_gate):
    depth = w_in.shape[0]
    b, s, d = x.shape
    alpha = (2 * depth) ** 0.25
    bf = jnp.bfloat16
    attn_cols = ROW_WI + N_IDX_HEADS

    cos_t, sin_t = _rope_tables(positions)

    for i in range(depth):
        wta = jnp.pad(w_in[i, :, :attn_cols].T.astype(bf),
                      ((0, ATTN_ROWS - attn_cols), (0, 0)))
        wsgu = w_in[i, :, attn_cols:].astype(bf)
        sgu_bias = jnp.repeat(sgu_b[i].T, SGU_GROUP_DIM, axis=1)

        qt, kn, vt, qit, kin, wit, bo = _proj(
            x, wta, wsgu, cos_t, sin_t, sgu_w[i], sgu_bias,
            sgu_ln_g[i][None], sgu_ln_b[i][None])
        at = _attn(qit, wit, qt, kin, kn, vt)
        x = _mix(x, at, bo, w_o[i].astype(bf), ln1_g[i][None], ln1_b[i][None], alpha)
        x2 = _ffn(x.reshape(b * s, d), p[i].reshape(b * s, PLE_DIM),
                  w_ffn_in[i].astype(bf), w_ffn_out[i].astype(bf),
                  w_ple[i].astype(bf), w_ple_gate[i].astype(bf),
                  ln2_g[i][None], ln2_b[i][None], alpha)
        x = x2.reshape(b, s, d)
    return x
```

```python
import functools

import jax
import jax.numpy as jnp
from jax import lax
from jax.experimental import pallas as pl
from jax.experimental.pallas import tpu as pltpu

D_MODEL = 1024
CHUNK = 64
N_HEADS = 8
N_KV_HEADS = 2
HEAD_DIM = 64
ATTN_WIDTH = N_HEADS * HEAD_DIM
ROPE_DIM = HEAD_DIM // 4
ROPE_HALF = ROPE_DIM // 2
ROPE_THETA = 500000.0
N_IDX_HEADS = 4
IDX_DIM = 64
TOPK_MAX = 256
SGU_CHUNK = 128
SGU_GROUPS = 8
SGU_WIDTH = D_MODEL - ATTN_WIDTH
SGU_GROUP_DIM = SGU_WIDTH // SGU_GROUPS
D_FF = 2816
PLE_DIM = 256
LN_EPS = 1e-5

Q_COLS = N_HEADS * HEAD_DIM
KV_COLS = N_KV_HEADS * HEAD_DIM
IQ_COLS = N_IDX_HEADS * IDX_DIM

ROW_Q = 0
ROW_K = ROW_Q + Q_COLS
ROW_V = ROW_K + KV_COLS
ROW_QI = ROW_V + KV_COLS
ROW_KI = ROW_QI + IQ_COLS
ROW_WI = ROW_KI + IDX_DIM
ATTN_ROWS = 1152

LANES = 128
SUBLANES = 8
VMEM_LIMIT = 56 * 1024 * 1024

PROJ_TM = 512
MIX_TM = 512
FFN_TM = 512
FFN_CHUNK = 256
ATTN_QB = 256
ATTN_KT = 256
ATTN_HEADS_PER_PASS = 2
KEY_BITS = 32
KEY_OF_NEG_INF = -0x7F800001
NEG_INF = float("-inf")


def _layer_norm(y, g, b):
    mu = jnp.mean(y, axis=-1, keepdims=True)
    d = y - mu
    var = jnp.mean(d * d, axis=-1, keepdims=True)
    return d * lax.rsqrt(var + LN_EPS) * g + b


def _dot(a, b):
    return jnp.dot(a, b, preferred_element_type=jnp.float32)


def _dot_nt(a, b):
    return lax.dot_general(a, b, (((1,), (1,)), ((), ())),
                           preferred_element_type=jnp.float32)


def _rope_table_kernel(pos_ref, cos_ref, sin_ref):
    seq = pos_ref.shape[-1]
    pos = pos_ref[0].astype(jnp.float32)
    i = lax.broadcasted_iota(jnp.int32, (ROPE_HALF, seq), 0).astype(jnp.float32)
    inv = jnp.power(jnp.float32(ROPE_THETA), i * (-2.0 / ROPE_DIM))
    ang = pos * inv
    cos_ref[0] = jnp.cos(ang)
    sin_ref[0] = jnp.sin(ang)


def _rope_tables(positions):
    b, s = positions.shape
    pos3 = positions.reshape(b, 1, s)
    out = jax.ShapeDtypeStruct((b, ROPE_HALF, s), jnp.float32)
    return pl.pallas_call(
        _rope_table_kernel,
        grid=(b,),
        in_specs=[pl.BlockSpec((1, 1, s), lambda i: (i, 0, 0))],
        out_specs=[pl.BlockSpec((1, ROPE_HALF, s), lambda i: (i, 0, 0))] * 2,
        out_shape=[out, out],
        name="rope_tables",
    )(pos3)


def _rope_rows(ht, cos, sin):
    x1 = ht[0:ROPE_HALF]
    x2 = ht[ROPE_HALF:ROPE_DIM]
    return jnp.concatenate(
        [x1 * cos - x2 * sin, x2 * cos + x1 * sin, ht[ROPE_DIM:]], axis=0)


def _proj_kernel(x_ref, wta_ref, wsgu_ref, cos_ref, sin_ref, sw_ref, sb_ref,
                 lng_ref, lnb_ref,
                 qt_ref, kn_ref, vt_ref, qit_ref, kin_ref, wit_ref, bo_ref):
    tm = x_ref.shape[1]
    xb = x_ref[0].astype(jnp.bfloat16)
    cos = cos_ref[0]
    sin = sin_ref[0]

    qt = _dot_nt(wta_ref[ROW_Q:ROW_K, :], xb)
    for h in range(N_HEADS):
        head = _rope_rows(qt[h * HEAD_DIM:(h + 1) * HEAD_DIM], cos, sin)
        qt_ref[0, h * HEAD_DIM:(h + 1) * HEAD_DIM, :] = (
            head * (HEAD_DIM ** -0.5)).astype(qt_ref.dtype)

    kt = _dot_nt(wta_ref[ROW_K:ROW_V, :], xb)
    kt = jnp.concatenate(
        [_rope_rows(kt[g * HEAD_DIM:(g + 1) * HEAD_DIM], cos, sin)
         for g in range(N_KV_HEADS)], axis=0)
    kn = kt.T

    vt = _dot_nt(wta_ref[ROW_V:ROW_QI, :], xb)

    qit = _dot_nt(wta_ref[ROW_QI:ROW_KI, :], xb)
    for h in range(N_IDX_HEADS):
        head = _rope_rows(qit[h * IDX_DIM:(h + 1) * IDX_DIM], cos, sin)
        qit_ref[0, h * IDX_DIM:(h + 1) * IDX_DIM, :] = head.astype(qit_ref.dtype)

    kwt = _dot_nt(wta_ref[ROW_KI:ATTN_ROWS, :], xb)
    kit = _rope_rows(kwt[0:IDX_DIM], cos, sin)
    kit = jnp.concatenate([kit, jnp.zeros_like(kit)], axis=0)
    kin = kit.T
    wit_ref[0] = kwt[IDX_DIM:IDX_DIM + SUBLANES] * ((N_IDX_HEADS * IDX_DIM) ** -0.5)

    for c in range(tm // ATTN_KT):
        rows = slice(c * ATTN_KT, (c + 1) * ATTN_KT)
        kn_ref[0, c] = kn[rows].astype(kn_ref.dtype)
        kin_ref[0, c] = kin[rows].astype(kin_ref.dtype)
        vt_ref[0, c] = vt[:, rows].astype(vt_ref.dtype)

    uv = _dot(xb, wsgu_ref[...])
    gu = jax.nn.gelu(uv[:, :SGU_WIDTH])
    gv = jax.nn.gelu(uv[:, SGU_WIDTH:])
    gv = _layer_norm(gv, lng_ref[...], lnb_ref[...]).astype(jnp.bfloat16)

    row = lax.broadcasted_iota(jnp.int32, (SGU_CHUNK, SGU_CHUNK), 0)
    col = lax.broadcasted_iota(jnp.int32, (SGU_CHUNK, SGU_CHUNK), 1)
    tri = col <= row
    ws = [jnp.where(tri, sw_ref[g], 0.0).astype(jnp.bfloat16)
          for g in range(SGU_GROUPS)]
    lane = lax.broadcasted_iota(jnp.int32, (SGU_CHUNK, LANES), 1)
    first_group = lane < SGU_GROUP_DIM
    for c in range(tm // SGU_CHUNK):
        rows = slice(c * SGU_CHUNK, (c + 1) * SGU_CHUNK)
        for pr in range(SGU_WIDTH // LANES):
            cols = slice(pr * LANES, (pr + 1) * LANES)
            vblk = gv[rows, cols]
            mixed = jnp.where(first_group,
                              _dot(ws[2 * pr], vblk), _dot(ws[2 * pr + 1], vblk))
            mixed = mixed + sb_ref[:, cols]
            bo_ref[0, rows, cols] = (gu[rows, cols] * mixed).astype(bo_ref.dtype)


def _proj(x, wta, wsgu, cos_t, sin_t, sgu_w, sgu_bias, ln_g, ln_b):
    b, s, d = x.shape
    tm = PROJ_TM
    nt = s // ATTN_KT
    tpt = tm // ATTN_KT
    grid = (b, s // tm)
    const2 = lambda i, j: (0, 0)
    in_specs = [
        pl.BlockSpec((1, tm, d), lambda i, j: (i, j, 0)),
        pl.BlockSpec(wta.shape, const2),
        pl.BlockSpec(wsgu.shape, const2),
        pl.BlockSpec((1, ROPE_HALF, tm), lambda i, j: (i, 0, j)),
        pl.BlockSpec((1, ROPE_HALF, tm), lambda i, j: (i, 0, j)),
        pl.BlockSpec(sgu_w.shape, lambda i, j: (0, 0, 0)),
        pl.BlockSpec(sgu_bias.shape, const2),
        pl.BlockSpec(ln_g.shape, const2),
        pl.BlockSpec(ln_b.shape, const2),
    ]
    key_rows = pl.BlockSpec((1, tpt, ATTN_KT, LANES), lambda i, j: (i, j, 0, 0))
    key_cols = pl.BlockSpec((1, tpt, LANES, ATTN_KT), lambda i, j: (i, j, 0, 0))
    out_specs = [
        pl.BlockSpec((1, Q_COLS, tm), lambda i, j: (i, 0, j)),
        key_rows,
        key_cols,
        pl.BlockSpec((1, IQ_COLS, tm), lambda i, j: (i, 0, j)),
        key_rows,
        pl.BlockSpec((1, SUBLANES, tm), lambda i, j: (i, 0, j)),
        pl.BlockSpec((1, tm, SGU_WIDTH), lambda i, j: (i, j, 0)),
    ]
    bf = jnp.bfloat16
    out_shape = [
        jax.ShapeDtypeStruct((b, Q_COLS, s), bf),
        jax.ShapeDtypeStruct((b, nt, ATTN_KT, LANES), bf),
        jax.ShapeDtypeStruct((b, nt, LANES, ATTN_KT), bf),
        jax.ShapeDtypeStruct((b, IQ_COLS, s), bf),
        jax.ShapeDtypeStruct((b, nt, ATTN_KT, LANES), bf),
        jax.ShapeDtypeStruct((b, SUBLANES, s), jnp.float32),
        jax.ShapeDtypeStruct((b, s, SGU_WIDTH), bf),
    ]
    return pl.pallas_call(
        _proj_kernel,
        grid=grid,
        in_specs=in_specs,
        out_specs=out_specs,
        out_shape=out_shape,
        compiler_params=pltpu.CompilerParams(
            dimension_semantics=("parallel", "parallel"),
            vmem_limit_bytes=VMEM_LIMIT),
        name="proj",
    )(x, wta, wsgu, cos_t, sin_t, sgu_w, sgu_bias, ln_g, ln_b)


def _key_to_float(key):
    bits = key ^ ((key >> 31) & jnp.int32(0x7FFFFFFF))
    return pltpu.bitcast(bits, jnp.float32)


def _ukey_to_float(ukey):
    key = jnp.maximum(ukey ^ jnp.int32(-0x80000000), KEY_OF_NEG_INF)
    return _key_to_float(key)


def _fold_sublanes(x, op):
    return op(x.reshape(x.shape[0] // SUBLANES, SUBLANES, x.shape[1]), axis=0)


def _attn_kernel(qit_ref, wit_ref, qt_ref, kin_ref, kn_ref, vt_ref, at_ref,
                 score_ref, bias_ref, s_ref, thr_ref, room_ref, qpad_ref, o_ref):
    j = pl.program_id(1)
    nt = j + 1
    topk = TOPK_MAX
    vshape = (SUBLANES, ATTN_QB)

    zeros_i = jnp.zeros((LANES - IDX_DIM, ATTN_QB), jnp.bfloat16)
    qi_pad = [jnp.concatenate([qit_ref[0, h * IDX_DIM:(h + 1) * IDX_DIM, :], zeros_i], axis=0)
              for h in range(N_IDX_HEADS)]
    wit = wit_ref[0]

    def tile_scores(c):
        kt = kin_ref[0, c]
        sc = jnp.zeros((ATTN_KT, ATTN_QB), jnp.float32)
        for h in range(N_IDX_HEADS):
            sc = sc + jnp.maximum(_dot(kt, qi_pad[h]), 0.0) * wit[h:h + 1, :]
        return sc

    def score_body(c, carry):
        score_ref[c] = tile_scores(c)
        return carry

    lax.fori_loop(0, j, score_body, 0)
    kchunk = lax.broadcasted_iota(jnp.int32, (ATTN_KT, ATTN_QB), 0) // CHUNK
    qchunk = lax.broadcasted_iota(jnp.int32, (ATTN_KT, ATTN_QB), 1) // CHUNK
    score_ref[j] = jnp.where(kchunk > qchunk, NEG_INF, tile_scores(j))

    @pl.when(nt * ATTN_KT <= topk)
    def _():
        def body(c, carry):
            bias_ref[c] = jnp.where(score_ref[c] == NEG_INF, NEG_INF, 0.0)
            return carry
        lax.fori_loop(0, nt, body, 0)

    def search(n_tiles):
        def count(pred):
            parts = [_fold_sublanes(jnp.where(pred(score_ref[c]), 1.0, 0.0), jnp.sum)
                     for c in range(n_tiles)]
            acc = functools.reduce(lambda a, b: a + b, parts)
            return jnp.broadcast_to(jnp.sum(acc, axis=0, keepdims=True), vshape)

        def bit_body(i, ukey):
            trial = ukey | jnp.left_shift(jnp.int32(1), (KEY_BITS - 1) - i)
            cand = _ukey_to_float(trial)[0:1, :]
            cnt = count(lambda blk: blk >= cand)
            return jnp.where(cnt >= topk, trial, ukey)

        ukey = lax.fori_loop(0, KEY_BITS, bit_body, jnp.zeros(vshape, jnp.int32))
        thr = _ukey_to_float(ukey)
        thr_ref[...] = thr
        room_ref[...] = topk - count(lambda blk: blk > thr[0:1, :])

    for n_tiles in range(topk // ATTN_KT + 1, score_ref.shape[0] + 1):
        pl.when(nt == n_tiles)(functools.partial(search, n_tiles))

    @pl.when(nt * ATTN_KT > topk)
    def _():
        thr = thr_ref[0:1, :]
        room = room_ref[0:1, :]
        krow = lax.broadcasted_iota(jnp.int32, (ATTN_KT, ATTN_KT), 0)
        kcol = lax.broadcasted_iota(jnp.int32, (ATTN_KT, ATTN_KT), 1)
        incl = (kcol <= krow).astype(jnp.bfloat16)

        def tie_body(c, seen):
            sc = score_ref[c]
            tie = sc == thr
            rank = _dot(incl, jnp.where(tie, 1.0, 0.0).astype(jnp.bfloat16)) + seen
            keep_tie = jnp.where(tie, jnp.where(rank <= room, 0.0, NEG_INF), NEG_INF)
            bias = jnp.where(sc > thr, 0.0, keep_tie)
            bias_ref[c] = jnp.where(sc == NEG_INF, NEG_INF, bias)
            return rank[ATTN_KT - 1:ATTN_KT, :]

        lax.fori_loop(0, nt, tie_body, jnp.zeros((1, ATTN_QB), jnp.float32))

    rep = N_HEADS // N_KV_HEADS
    hpp = ATTN_HEADS_PER_PASS
    width = hpp * ATTN_QB
    n_pass = N_HEADS // hpp
    pass_cols = [slice(ps * width, (ps + 1) * width) for ps in range(n_pass)]
    pass_group = [(ps * hpp) // rep for ps in range(n_pass)]

    qpad_ref[...] = jnp.zeros(qpad_ref.shape, qpad_ref.dtype)
    for h in range(N_HEADS):
        g = h // rep
        qpad_ref[g * HEAD_DIM:(g + 1) * HEAD_DIM, h * ATTN_QB:(h + 1) * ATTN_QB] = (
            qt_ref[0, h * HEAD_DIM:(h + 1) * HEAD_DIM, :])

    def qk_body(c, m_accs):
        kt = kn_ref[0, c]
        bias = jnp.concatenate([bias_ref[c]] * hpp, axis=1)
        out = []
        for ps in range(n_pass):
            s = _dot(kt, qpad_ref[:, pass_cols[ps]]) + bias
            s_ref[c, :, pass_cols[ps]] = s
            out.append(jnp.maximum(m_accs[ps], _fold_sublanes(s, jnp.max)))
        return tuple(out)

    m_accs = lax.fori_loop(
        0, nt, qk_body,
        tuple(jnp.full((SUBLANES, width), NEG_INF, jnp.float32) for _ in range(n_pass)))
    m_rows = [jnp.max(m, axis=0, keepdims=True) for m in m_accs]

    o_ref[...] = jnp.zeros(o_ref.shape, o_ref.dtype)

    def pv_body(c, l_accs):
        out = []
        for ps in range(n_pass):
            g = pass_group[ps]
            p = jnp.exp(s_ref[c, :, pass_cols[ps]] - m_rows[ps])
            out.append(l_accs[ps] + _fold_sublanes(p, jnp.sum))
            v_blk = vt_ref[0, c, g * HEAD_DIM:(g + 1) * HEAD_DIM, :]
            o_ref[ps] += _dot(v_blk, p.astype(jnp.bfloat16))
        return tuple(out)

    l_accs = lax.fori_loop(
        0, nt, pv_body,
        tuple(jnp.zeros((SUBLANES, width), jnp.float32) for _ in range(n_pass)))
    for ps in range(n_pass):
        out = o_ref[ps] / jnp.sum(l_accs[ps], axis=0, keepdims=True)
        for r in range(hpp):
            h = ps * hpp + r
            at_ref[0, h * HEAD_DIM:(h + 1) * HEAD_DIM, :] = (
                out[:, r * ATTN_QB:(r + 1) * ATTN_QB].astype(at_ref.dtype))


def _attn(qit, wit, qt, kin, kn, vt):
    b, _, s = qt.shape
    nt = s // ATTN_KT
    per_q = lambda rows: pl.BlockSpec((1, rows, ATTN_QB), lambda i, j: (i, 0, j))
    per_b = lambda arr: pl.BlockSpec((1,) + arr.shape[1:], lambda i, j: (i, 0, 0, 0))
    return pl.pallas_call(
        _attn_kernel,
        grid=(b, s // ATTN_QB),
        in_specs=[per_q(IQ_COLS), per_q(SUBLANES), per_q(Q_COLS),
                  per_b(kin), per_b(kn), per_b(vt)],
        out_specs=per_q(ATTN_WIDTH),
        out_shape=jax.ShapeDtypeStruct((b, ATTN_WIDTH, s), jnp.bfloat16),
        scratch_shapes=[
            pltpu.VMEM((nt, ATTN_KT, ATTN_QB), jnp.float32),
            pltpu.VMEM((nt, ATTN_KT, ATTN_QB), jnp.float32),
            pltpu.VMEM((nt, ATTN_KT, N_HEADS * ATTN_QB), jnp.float32),
            pltpu.VMEM((SUBLANES, ATTN_QB), jnp.float32),
            pltpu.VMEM((SUBLANES, ATTN_QB), jnp.float32),
            pltpu.VMEM((N_KV_HEADS * HEAD_DIM, N_HEADS * ATTN_QB), jnp.bfloat16),
            pltpu.VMEM((N_HEADS // ATTN_HEADS_PER_PASS, HEAD_DIM,
                        ATTN_HEADS_PER_PASS * ATTN_QB), jnp.float32),
        ],
        compiler_params=pltpu.CompilerParams(
            dimension_semantics=("parallel", "arbitrary"),
            vmem_limit_bytes=VMEM_LIMIT),
        name="attn",
    )(qit, wit, qt, kin, kn, vt)


def _mix_kernel(alpha, x_ref, at_ref, bo_ref, wo_ref, g_ref, b_ref, o_ref):
    a_part = lax.dot_general(at_ref[0], wo_ref[0:ATTN_WIDTH, :],
                             (((0,), (0,)), ((), ())),
                             preferred_element_type=jnp.float32)
    b_part = _dot(bo_ref[0], wo_ref[ATTN_WIDTH:, :])
    y = alpha * x_ref[0] + a_part + b_part
    o_ref[0] = _layer_norm(y, g_ref[...], b_ref[...])


def _mix(x, at, bo, wo, g, bta, alpha):
    b, s, d = x.shape
    tm = MIX_TM
    const2 = lambda i, j: (0, 0)
    return pl.pallas_call(
        functools.partial(_mix_kernel, alpha),
        grid=(b, s // tm),
        in_specs=[
            pl.BlockSpec((1, tm, d), lambda i, j: (i, j, 0)),
            pl.BlockSpec((1, ATTN_WIDTH, tm), lambda i, j: (i, 0, j)),
            pl.BlockSpec((1, tm, SGU_WIDTH), lambda i, j: (i, j, 0)),
            pl.BlockSpec(wo.shape, const2),
            pl.BlockSpec(g.shape, const2),
            pl.BlockSpec(bta.shape, const2),
        ],
        out_specs=pl.BlockSpec((1, tm, d), lambda i, j: (i, j, 0)),
        out_shape=jax.ShapeDtypeStruct(x.shape, x.dtype),
        compiler_params=pltpu.CompilerParams(
            dimension_semantics=("parallel", "parallel"),
            vmem_limit_bytes=VMEM_LIMIT),
        name="mix",
    )(x, at, bo, wo, g, bta)


def _ffn_kernel(alpha, x_ref, p_ref, win_ref, wout_ref, wple_ref, wgate_ref,
                g_ref, b_ref, o_ref, acc_ref):
    x = x_ref[...]
    xb = x.astype(jnp.bfloat16)
    gate_lin = _dot(xb, wgate_ref[...])
    ple = _dot(p_ref[...].astype(jnp.bfloat16), wple_ref[...]) * jax.nn.sigmoid(gate_lin)
    acc_ref[...] = alpha * x + ple
    for c in range(D_FF // FFN_CHUNK):
        cols = slice(c * FFN_CHUNK, (c + 1) * FFN_CHUNK)
        up_cols = slice(D_FF + c * FFN_CHUNK, D_FF + (c + 1) * FFN_CHUNK)
        gate = _dot(xb, win_ref[:, cols])
        up = _dot(xb, win_ref[:, up_cols])
        h = (jax.nn.silu(gate) * up).astype(jnp.bfloat16)
        acc_ref[...] += _dot(h, wout_ref[cols, :])
    o_ref[...] = _layer_norm(acc_ref[...], g_ref[...], b_ref[...])


def _ffn(x2, p2, win, wout, wple, wgate, g, bta, alpha):
    n, d = x2.shape
    tm = FFN_TM
    const2 = lambda i: (0, 0)
    return pl.pallas_call(
        functools.partial(_ffn_kernel, alpha),
        grid=(n // tm,),
        in_specs=[
            pl.BlockSpec((tm, d), lambda i: (i, 0)),
            pl.BlockSpec((tm, PLE_DIM), lambda i: (i, 0)),
            pl.BlockSpec(win.shape, const2, pipeline_mode=pl.Buffered(1)),
            pl.BlockSpec(wout.shape, const2, pipeline_mode=pl.Buffered(1)),
            pl.BlockSpec(wple.shape, const2, pipeline_mode=pl.Buffered(1)),
            pl.BlockSpec(wgate.shape, const2, pipeline_mode=pl.Buffered(1)),
            pl.BlockSpec(g.shape, const2),
            pl.BlockSpec(bta.shape, const2),
        ],
        out_specs=pl.BlockSpec((tm, d), lambda i: (i, 0)),
        out_shape=jax.ShapeDtypeStruct(x2.shape, x2.dtype),
        scratch_shapes=[pltpu.VMEM((tm, d), jnp.float32)],
        compiler_params=pltpu.CompilerParams(
            dimension_semantics=("parallel",),
            vmem_limit_bytes=VMEM_LIMIT),
        name="ffn",
    )(x2, p2, win, wout, wple, wgate, g, bta)


def kernel(x, p, positions, w_in, w_o, ln1_g, ln1_b, ln2_g, ln2_b, sgu_w, sgu_b,
           sgu_ln_g, sgu_ln_b, w_ffn_in, w_ffn_out, w_ple, w_ple_gate):
    depth = w_in.shape[0]
    b, s, d = x.shape
    alpha = (2 * depth) ** 0.25
    bf = jnp.bfloat16
    attn_cols = ROW_WI + N_IDX_HEADS

    cos_t, sin_t = _rope_tables(positions)

    for i in range(depth):
        wta = jnp.pad(w_in[i, :, :attn_cols].T.astype(bf),
                      ((0, ATTN_ROWS - attn_cols), (0, 0)))
        wsgu = w_in[i, :, attn_cols:].astype(bf)
        sgu_bias = jnp.repeat(sgu_b[i].T, SGU_GROUP_DIM, axis=1)

        qt, kn, vt, qit, kin, wit, bo = _proj(
            x, wta, wsgu, cos_t, sin_t, sgu_w[i], sgu_bias,
            sgu_ln_g[i][None], sgu_ln_b[i][None])
        at = _attn(qit, wit, qt, kin, kn, vt)
        x = _mix(x, at, bo, w_o[i].astype(bf), ln1_g[i][None], ln1_b[i][None], alpha)
        x2 = _ffn(x.reshape(b * s, d), p[i].reshape(b * s, PLE_DIM),
                  w_ffn_in[i].astype(bf), w_ffn_out[i].astype(bf),
                  w_ple[i].astype(bf), w_ple_gate[i].astype(bf),
                  ln2_g[i][None], ln2_b[i][None], alpha)
        x = x2.reshape(b, s, d)
    return x
```

```python
import functools
import math

import jax
import jax.numpy as jnp
from jax import lax
from jax.experimental import pallas as pl
from jax.experimental.pallas import tpu as pltpu

D_MODEL = 1024
CHUNK = 64
N_HEADS = 8
N_KV_HEADS = 2
HEAD_DIM = 64
ATTN_WIDTH = N_HEADS * HEAD_DIM
ROPE_DIM = HEAD_DIM // 4
ROPE_HALF = ROPE_DIM // 2
ROPE_THETA = 500000.0
N_IDX_HEADS = 4
IDX_DIM = 64
TOPK_MAX = 256
SGU_CHUNK = 128
SGU_GROUPS = 8
SGU_WIDTH = D_MODEL - ATTN_WIDTH
SGU_GROUP_DIM = SGU_WIDTH // SGU_GROUPS
D_FF = 2816
PLE_DIM = 256
LN_EPS = 1e-5

Q_COLS = N_HEADS * HEAD_DIM
KV_COLS = N_KV_HEADS * HEAD_DIM
IQ_COLS = N_IDX_HEADS * IDX_DIM

ROW_Q = 0
ROW_K = ROW_Q + Q_COLS
ROW_V = ROW_K + KV_COLS
ROW_QI = ROW_V + KV_COLS
ROW_KI = ROW_QI + IQ_COLS
ROW_WI = ROW_KI + IDX_DIM
ATTN_ROWS = 1152

LANES = 128
SUBLANES = 8
BF16_ROWS = 16
VMEM_LIMIT = 56 * 1024 * 1024

PROJ_TM = 512
FFN_TM = 512
FFN_CHUNK = 256
ATTN_QB = 256
ATTN_KT = 256
ATTN_HEADS_PER_PASS = 2
TILE_UNROLL = 2
HI_BITS = 16
DIGIT_BITS = 8
KEY_OF_NEG_INF = -0x7F800001
HI_KEY_OF_NEG_INF = -0x7F810000
MIN_NORMAL_KEY = 0x00800000
INT_MIN = -0x80000000
NEG_INF = float("-inf")
LOG2E = math.log2(math.e)


def _layer_norm(y, g, b):
    mu = jnp.mean(y, axis=-1, keepdims=True)
    d = y - mu
    var = jnp.mean(d * d, axis=-1, keepdims=True)
    return d * lax.rsqrt(var + LN_EPS) * g + b


def _dot(a, b):
    return jnp.dot(a, b, preferred_element_type=jnp.float32)


def _dot_nt(a, b):
    return lax.dot_general(a, b, (((1,), (1,)), ((), ())),
                           preferred_element_type=jnp.float32)


def _dot_tn(a, b):
    return lax.dot_general(a, b, (((0,), (0,)), ((), ())),
                           preferred_element_type=jnp.float32)


def _rope_table_kernel(pos_ref, cos_ref, sin_ref):
    seq = pos_ref.shape[-1]
    pos = pos_ref[0].astype(jnp.float32)
    i = lax.broadcasted_iota(jnp.int32, (ROPE_HALF, seq), 0).astype(jnp.float32)
    inv = jnp.power(jnp.float32(ROPE_THETA), i * (-2.0 / ROPE_DIM))
    ang = pos * inv
    cos_ref[0] = jnp.cos(ang)
    sin_ref[0] = jnp.sin(ang)


def _rope_tables(positions):
    b, s = positions.shape
    pos3 = positions.reshape(b, 1, s)
    out = jax.ShapeDtypeStruct((b, ROPE_HALF, s), jnp.float32)
    return pl.pallas_call(
        _rope_table_kernel,
        grid=(b,),
        in_specs=[pl.BlockSpec((1, 1, s), lambda i: (i, 0, 0))],
        out_specs=[pl.BlockSpec((1, ROPE_HALF, s), lambda i: (i, 0, 0))] * 2,
        out_shape=[out, out],
        name="rope_tables",
    )(pos3)


def _rope_rows(ht, cos, sin):
    x1 = ht[0:ROPE_HALF]
    x2 = ht[ROPE_HALF:ROPE_DIM]
    return jnp.concatenate(
        [x1 * cos - x2 * sin, x2 * cos + x1 * sin, ht[ROPE_DIM:]], axis=0)


def _proj_kernel(x_ref, wta_ref, wsgu_ref, cos_ref, sin_ref, sw_ref, sb_ref,
                 lng_ref, lnb_ref,
                 qt_ref, kn_ref, vt_ref, qit_ref, kin_ref, wit_ref, bo_ref):
    tm = x_ref.shape[1]
    xb = x_ref[0].astype(jnp.bfloat16)
    cos = cos_ref[0]
    sin = sin_ref[0]

    q_scale = HEAD_DIM ** -0.5 * LOG2E
    qt = _dot_nt(wta_ref[ROW_Q:ROW_K, :], xb)
    for h in range(N_HEADS):
        head = _rope_rows(qt[h * HEAD_DIM:(h + 1) * HEAD_DIM], cos, sin)
        qt_ref[0, h * HEAD_DIM:(h + 1) * HEAD_DIM, :] = (head * q_scale).astype(qt_ref.dtype)

    kt = _dot_nt(wta_ref[ROW_K:ROW_V, :], xb)
    kt = jnp.concatenate(
        [_rope_rows(kt[g * HEAD_DIM:(g + 1) * HEAD_DIM], cos, sin)
         for g in range(N_KV_HEADS)], axis=0)
    kn = kt.T

    vt = _dot_nt(wta_ref[ROW_V:ROW_QI, :], xb)

    qit = _dot_nt(wta_ref[ROW_QI:ROW_KI, :], xb)
    for h in range(N_IDX_HEADS):
        head = _rope_rows(qit[h * IDX_DIM:(h + 1) * IDX_DIM], cos, sin)
        qit_ref[0, h * IDX_DIM:(h + 1) * IDX_DIM, :] = head.astype(qit_ref.dtype)

    kwt = _dot_nt(wta_ref[ROW_KI:ATTN_ROWS, :], xb)
    kit = _rope_rows(kwt[0:IDX_DIM], cos, sin)
    kit = jnp.concatenate([kit, jnp.zeros_like(kit)], axis=0)
    kin = kit.T
    wit_ref[0] = kwt[IDX_DIM:IDX_DIM + SUBLANES] * ((N_IDX_HEADS * IDX_DIM) ** -0.5)

    for c in range(tm // ATTN_KT):
        rows = slice(c * ATTN_KT, (c + 1) * ATTN_KT)
        kn_ref[0, c] = kn[rows].astype(kn_ref.dtype)
        kin_ref[0, c] = kin[rows].astype(kin_ref.dtype)
        vt_ref[0, c] = vt[:, rows].astype(vt_ref.dtype)

    uv = _dot(xb, wsgu_ref[...])
    gu = jax.nn.gelu(uv[:, :SGU_WIDTH])
    gv = jax.nn.gelu(uv[:, SGU_WIDTH:])
    gv = _layer_norm(gv, lng_ref[...], lnb_ref[...]).astype(jnp.bfloat16)

    row = lax.broadcasted_iota(jnp.int32, (SGU_CHUNK, SGU_CHUNK), 0)
    col = lax.broadcasted_iota(jnp.int32, (SGU_CHUNK, SGU_CHUNK), 1)
    tri = col <= row
    ws = [jnp.where(tri, sw_ref[g], 0.0).astype(jnp.bfloat16)
          for g in range(SGU_GROUPS)]
    lane = lax.broadcasted_iota(jnp.int32, (SGU_CHUNK, LANES), 1)
    first_group = lane < SGU_GROUP_DIM
    for c in range(tm // SGU_CHUNK):
        rows = slice(c * SGU_CHUNK, (c + 1) * SGU_CHUNK)
        for pr in range(SGU_WIDTH // LANES):
            cols = slice(pr * LANES, (pr + 1) * LANES)
            vblk = gv[rows, cols]
            mixed = jnp.where(first_group,
                              _dot(ws[2 * pr], vblk), _dot(ws[2 * pr + 1], vblk))
            mixed = mixed + sb_ref[:, cols]
            bo_ref[0, rows, cols] = (gu[rows, cols] * mixed).astype(bo_ref.dtype)


def _proj(x, wta, wsgu, cos_t, sin_t, sgu_w, sgu_bias, ln_g, ln_b):
    b, s, d = x.shape
    tm = PROJ_TM
    nt = s // ATTN_KT
    tpt = tm // ATTN_KT
    grid = (b, s // tm)
    const2 = lambda i, j: (0, 0)
    in_specs = [
        pl.BlockSpec((1, tm, d), lambda i, j: (i, j, 0)),
        pl.BlockSpec(wta.shape, const2),
        pl.BlockSpec(wsgu.shape, const2),
        pl.BlockSpec((1, ROPE_HALF, tm), lambda i, j: (i, 0, j)),
        pl.BlockSpec((1, ROPE_HALF, tm), lambda i, j: (i, 0, j)),
        pl.BlockSpec(sgu_w.shape, lambda i, j: (0, 0, 0)),
        pl.BlockSpec(sgu_bias.shape, const2),
        pl.BlockSpec(ln_g.shape, const2),
        pl.BlockSpec(ln_b.shape, const2),
    ]
    key_rows = pl.BlockSpec((1, tpt, ATTN_KT, LANES), lambda i, j: (i, j, 0, 0))
    key_cols = pl.BlockSpec((1, tpt, LANES, ATTN_KT), lambda i, j: (i, j, 0, 0))
    out_specs = [
        pl.BlockSpec((1, Q_COLS, tm), lambda i, j: (i, 0, j)),
        key_rows,
        key_cols,
        pl.BlockSpec((1, IQ_COLS, tm), lambda i, j: (i, 0, j)),
        key_rows,
        pl.BlockSpec((1, SUBLANES, tm), lambda i, j: (i, 0, j)),
        pl.BlockSpec((1, tm, SGU_WIDTH), lambda i, j: (i, j, 0)),
    ]
    bf = jnp.bfloat16
    out_shape = [
        jax.ShapeDtypeStruct((b, Q_COLS, s), bf),
        jax.ShapeDtypeStruct((b, nt, ATTN_KT, LANES), bf),
        jax.ShapeDtypeStruct((b, nt, LANES, ATTN_KT), bf),
        jax.ShapeDtypeStruct((b, IQ_COLS, s), bf),
        jax.ShapeDtypeStruct((b, nt, ATTN_KT, LANES), bf),
        jax.ShapeDtypeStruct((b, SUBLANES, s), jnp.float32),
        jax.ShapeDtypeStruct((b, s, SGU_WIDTH), bf),
    ]
    return pl.pallas_call(
        _proj_kernel,
        grid=grid,
        in_specs=in_specs,
        out_specs=out_specs,
        out_shape=out_shape,
        compiler_params=pltpu.CompilerParams(
            dimension_semantics=("parallel", "parallel"),
            vmem_limit_bytes=VMEM_LIMIT),
        name="proj",
    )(x, wta, wsgu, cos_t, sin_t, sgu_w, sgu_bias, ln_g, ln_b)


def _fold_sublanes(x, op):
    return op(x.reshape(x.shape[0] // SUBLANES, SUBLANES, x.shape[1]), axis=0)


def _tile_loop(lo, hi, body, init):
    def group(i, carry):
        for u in range(TILE_UNROLL):
            carry = body(lo + i * TILE_UNROLL + u, carry)
        return carry
    n_groups = (hi - lo) // TILE_UNROLL
    carry = lax.fori_loop(0, n_groups, group, init)
    return lax.fori_loop(lo + n_groups * TILE_UNROLL, hi, body, carry)


def _count_bf16(ref, n_tiles, pred):
    one = jnp.ones((), jnp.bfloat16)
    zero = jnp.zeros((), jnp.bfloat16)
    groups = ATTN_KT // BF16_ROWS
    n_acc = 4
    assert groups * n_tiles <= 256 * n_acc
    accs = [None] * n_acc
    for c in range(n_tiles):
        m3 = jnp.where(pred(ref[c]), one, zero).reshape(groups, BF16_ROWS, ATTN_QB)
        for i in range(groups):
            k = i % n_acc
            accs[k] = m3[i] if accs[k] is None else accs[k] + m3[i]
    acc = (accs[0] + accs[1]) + (accs[2] + accs[3])
    return jnp.sum(acc.astype(jnp.float32), axis=0, keepdims=True)


def _bit_search(n_bits, count_ge, target):
    def body(i, val):
        trial = val | jnp.left_shift(jnp.int32(1), (n_bits - 1) - i)
        return jnp.where(count_ge(trial) >= target, trial, val)
    return lax.fori_loop(0, n_bits, body, jnp.zeros((1, ATTN_QB), jnp.int32))


def _hi_to_bf16(uhi):
    key = jnp.maximum(jnp.left_shift(uhi, HI_BITS) ^ jnp.int32(INT_MIN), HI_KEY_OF_NEG_INF)
    key = jnp.where(key > 0, jnp.maximum(key, MIN_NORMAL_KEY), key)
    bits = key ^ ((key >> 31) & jnp.int32(0x7FFF0000))
    return pltpu.bitcast(bits, jnp.float32).astype(jnp.bfloat16)


def _digit_to_bf16(d):
    return d.astype(jnp.float32).astype(jnp.bfloat16)


def _threshold_search(n_tiles, key_ref, hb_ref, dg_ref, thr_ref, room_ref):
    digit_mask = (1 << DIGIT_BITS) - 1
    minus1 = jnp.full((), -1, jnp.bfloat16)
    target = jnp.float32(TOPK_MAX)

    def count_ge(ref, cand):
        return _count_bf16(ref, n_tiles, lambda blk: blk >= cand)

    uhi = _bit_search(HI_BITS, lambda t: count_ge(hb_ref, _hi_to_bf16(t)), target)
    hi_f = _hi_to_bf16(uhi)
    target = target - _count_bf16(hb_ref, n_tiles, lambda blk: blk > hi_f)

    for c in range(n_tiles):
        dig = _digit_to_bf16((key_ref[c] >> DIGIT_BITS) & digit_mask)
        dg_ref[c] = jnp.where(hb_ref[c] == hi_f, dig, minus1)
    d1 = _bit_search(DIGIT_BITS, lambda t: count_ge(dg_ref, _digit_to_bf16(t)), target)
    d1_f = _digit_to_bf16(d1)
    target = target - _count_bf16(dg_ref, n_tiles, lambda blk: blk > d1_f)

    for c in range(n_tiles):
        dig = _digit_to_bf16(key_ref[c] & digit_mask)
        hb_ref[c] = jnp.where(dg_ref[c] == d1_f, dig, minus1)
    d0 = _bit_search(DIGIT_BITS, lambda t: count_ge(hb_ref, _digit_to_bf16(t)), target)
    d0_f = _digit_to_bf16(d0)
    room = target - _count_bf16(hb_ref, n_tiles, lambda blk: blk > d0_f)

    ukey = jnp.left_shift(uhi, HI_BITS) | jnp.left_shift(d1, DIGIT_BITS) | d0
    thr_ref[...] = jnp.broadcast_to(ukey ^ jnp.int32(INT_MIN), thr_ref.shape)
    room_ref[...] = jnp.broadcast_to(room, room_ref.shape)


def _attn_kernel(qit_ref, wit_ref, qt_ref, kin_ref, kn_ref, vt_ref, at_ref,
                 key_ref, hb_ref, dg_ref, bias_ref, s_ref, thr_ref, room_ref,
                 qpad_ref, o_ref):
    j = pl.program_id(1)
    nt = j + 1
    max_tiles = key_ref.shape[0]

    zeros_i = jnp.zeros((LANES - IDX_DIM, ATTN_QB), jnp.bfloat16)
    qi_pad = [jnp.concatenate([qit_ref[0, h * IDX_DIM:(h + 1) * IDX_DIM, :], zeros_i], axis=0)
              for h in range(N_IDX_HEADS)]
    wit = wit_ref[0]

    def tile_scores(c):
        kt = kin_ref[0, c]
        sc = jnp.zeros((ATTN_KT, ATTN_QB), jnp.float32)
        for h in range(N_IDX_HEADS):
            sc = sc + jnp.maximum(_dot(kt, qi_pad[h]), 0.0) * wit[h:h + 1, :]
        return sc

    def store_scores(c, sc):
        bits = pltpu.bitcast(sc, jnp.int32)
        bits = jnp.where((bits & jnp.int32(0x7F800000)) == 0, 0, bits)
        key_ref[c] = bits ^ ((bits >> 31) & jnp.int32(0x7FFFFFFF))
        hb_ref[c] = pltpu.bitcast(bits & jnp.int32(-0x10000), jnp.float32).astype(jnp.bfloat16)

    def score_body(c, carry):
        store_scores(c, tile_scores(c))
        return carry

    _tile_loop(0, j, score_body, 0)
    kchunk = lax.broadcasted_iota(jnp.int32, (ATTN_KT, ATTN_QB), 0) // CHUNK
    qchunk = lax.broadcasted_iota(jnp.int32, (ATTN_KT, ATTN_QB), 1) // CHUNK
    store_scores(j, jnp.where(kchunk > qchunk, NEG_INF, tile_scores(j)))

    @pl.when(nt * ATTN_KT <= TOPK_MAX)
    def _():
        def body(c, carry):
            bias_ref[c] = jnp.where(key_ref[c] == KEY_OF_NEG_INF, NEG_INF, 0.0)
            return carry
        lax.fori_loop(0, nt, body, 0)

    for n_tiles in range(TOPK_MAX // ATTN_KT + 1, max_tiles + 1):
        pl.when(nt == n_tiles)(functools.partial(
            _threshold_search, n_tiles, key_ref, hb_ref, dg_ref, thr_ref, room_ref))

    @pl.when(nt * ATTN_KT > TOPK_MAX)
    def _():
        thr = thr_ref[0:1, :]
        room = room_ref[0:1, :]
        krow = lax.broadcasted_iota(jnp.int32, (ATTN_KT, ATTN_KT), 0)
        kcol = lax.broadcasted_iota(jnp.int32, (ATTN_KT, ATTN_KT), 1)
        incl = (kcol <= krow).astype(jnp.bfloat16)

        def tie_body(c, seen):
            key = key_ref[c]
            tie = key == thr
            rank = _dot(incl, jnp.where(tie, 1.0, 0.0).astype(jnp.bfloat16)) + seen
            keep_tie = jnp.where(tie, jnp.where(rank <= room, 0.0, NEG_INF), NEG_INF)
            bias = jnp.where(key > thr, 0.0, keep_tie)
            bias_ref[c] = jnp.where(key == KEY_OF_NEG_INF, NEG_INF, bias)
            return rank[ATTN_KT - 1:ATTN_KT, :]

        _tile_loop(0, nt, tie_body, jnp.zeros((1, ATTN_QB), jnp.float32))

    rep = N_HEADS // N_KV_HEADS
    hpp = ATTN_HEADS_PER_PASS
    width = hpp * ATTN_QB
    n_pass = N_HEADS // hpp
    pass_cols = [slice(ps * width, (ps + 1) * width) for ps in range(n_pass)]
    pass_group = [(ps * hpp) // rep for ps in range(n_pass)]

    qpad_ref[...] = jnp.zeros(qpad_ref.shape, qpad_ref.dtype)
    for h in range(N_HEADS):
        g = h // rep
        qpad_ref[g * HEAD_DIM:(g + 1) * HEAD_DIM, h * ATTN_QB:(h + 1) * ATTN_QB] = (
            qt_ref[0, h * HEAD_DIM:(h + 1) * HEAD_DIM, :])

    def qk_body(c, m_accs):
        kt = kn_ref[0, c]
        bias = jnp.concatenate([bias_ref[c]] * hpp, axis=1)
        out = []
        for ps in range(n_pass):
            s = _dot(kt, qpad_ref[:, pass_cols[ps]]) + bias
            s_ref[c, :, pass_cols[ps]] = s
            out.append(jnp.maximum(m_accs[ps], _fold_sublanes(s, jnp.max)))
        return tuple(out)

    m_accs = _tile_loop(
        0, nt, qk_body,
        tuple(jnp.full((SUBLANES, width), NEG_INF, jnp.float32) for _ in range(n_pass)))
    m_rows = [jnp.max(m, axis=0, keepdims=True) for m in m_accs]

    o_ref[...] = jnp.zeros(o_ref.shape, o_ref.dtype)

    def pv_body(c, l_accs):
        out = []
        for ps in range(n_pass):
            g = pass_group[ps]
            p = jnp.exp2(s_ref[c, :, pass_cols[ps]] - m_rows[ps])
            out.append(l_accs[ps] + _fold_sublanes(p, jnp.sum))
            v_blk = vt_ref[0, c, g * HEAD_DIM:(g + 1) * HEAD_DIM, :]
            o_ref[ps] += _dot(v_blk, p.astype(jnp.bfloat16))
        return tuple(out)

    l_accs = _tile_loop(
        0, nt, pv_body,
        tuple(jnp.zeros((SUBLANES, width), jnp.float32) for _ in range(n_pass)))
    for ps in range(n_pass):
        out = o_ref[ps] / jnp.sum(l_accs[ps], axis=0, keepdims=True)
        for r in range(hpp):
            h = ps * hpp + r
            at_ref[0, h * HEAD_DIM:(h + 1) * HEAD_DIM, :] = (
                out[:, r * ATTN_QB:(r + 1) * ATTN_QB].astype(at_ref.dtype))


def _attn(qit, wit, qt, kin, kn, vt):
    b, _, s = qt.shape
    nt = s // ATTN_KT
    per_q = lambda rows: pl.BlockSpec((1, rows, ATTN_QB), lambda i, j: (i, 0, j))
    per_b = lambda arr: pl.BlockSpec((1,) + arr.shape[1:], lambda i, j: (i, 0, 0, 0))
    tile = (nt, ATTN_KT, ATTN_QB)
    return pl.pallas_call(
        _attn_kernel,
        grid=(b, s // ATTN_QB),
        in_specs=[per_q(IQ_COLS), per_q(SUBLANES), per_q(Q_COLS),
                  per_b(kin), per_b(kn), per_b(vt)],
        out_specs=per_q(ATTN_WIDTH),
        out_shape=jax.ShapeDtypeStruct((b, ATTN_WIDTH, s), jnp.bfloat16),
        scratch_shapes=[
            pltpu.VMEM(tile, jnp.int32),
            pltpu.VMEM(tile, jnp.bfloat16),
            pltpu.VMEM(tile, jnp.bfloat16),
            pltpu.VMEM(tile, jnp.float32),
            pltpu.VMEM((nt, ATTN_KT, N_HEADS * ATTN_QB), jnp.float32),
            pltpu.VMEM((SUBLANES, ATTN_QB), jnp.int32),
            pltpu.VMEM((SUBLANES, ATTN_QB), jnp.float32),
            pltpu.VMEM((N_KV_HEADS * HEAD_DIM, N_HEADS * ATTN_QB), jnp.bfloat16),
            pltpu.VMEM((N_HEADS // ATTN_HEADS_PER_PASS, HEAD_DIM,
                        ATTN_HEADS_PER_PASS * ATTN_QB), jnp.float32),
        ],
        compiler_params=pltpu.CompilerParams(
            dimension_semantics=("parallel", "arbitrary"),
            vmem_limit_bytes=VMEM_LIMIT),
        name="attn",
    )(qit, wit, qt, kin, kn, vt)


def _mixffn_kernel(alpha, x_ref, at_ref, bo_ref, p_ref, wo_ref, win_ref, wout_ref,
                   wple_ref, wgate_ref, g1_ref, b1_ref, g2_ref, b2_ref, o_ref, acc_ref):
    mix = _dot_tn(at_ref[0], wo_ref[0:ATTN_WIDTH, :]) + _dot(bo_ref[0], wo_ref[ATTN_WIDTH:, :])
    x = _layer_norm(alpha * x_ref[0] + mix, g1_ref[...], b1_ref[...])
    xb = x.astype(jnp.bfloat16)
    gate_lin = _dot(xb, wgate_ref[...])
    ple = _dot(p_ref[0].astype(jnp.bfloat16), wple_ref[...]) * jax.nn.sigmoid(gate_lin)
    acc_ref[...] = alpha * x + ple
    for c in range(D_FF // FFN_CHUNK):
        cols = slice(c * FFN_CHUNK, (c + 1) * FFN_CHUNK)
        up_cols = slice(D_FF + c * FFN_CHUNK, D_FF + (c + 1) * FFN_CHUNK)
        gate = _dot(xb, win_ref[:, cols])
        up = _dot(xb, win_ref[:, up_cols])
        h = (jax.nn.silu(gate) * up).astype(jnp.bfloat16)
        acc_ref[...] += _dot(h, wout_ref[cols, :])
    o_ref[0] = _layer_norm(acc_ref[...], g2_ref[...], b2_ref[...])


def _mixffn(x, at, bo, p, wo, win, wout, wple, wgate, g1, b1, g2, b2, alpha):
    b, s, d = x.shape
    tm = FFN_TM
    const2 = lambda i, j: (0, 0)
    resident = lambda w: pl.BlockSpec(w.shape, const2, pipeline_mode=pl.Buffered(1))
    rows = lambda width: pl.BlockSpec((1, tm, width), lambda i, j: (i, j, 0))
    small = lambda v: pl.BlockSpec(v.shape, const2)
    return pl.pallas_call(
        functools.partial(_mixffn_kernel, alpha),
        grid=(b, s // tm),
        in_specs=[
            rows(d),
            pl.BlockSpec((1, ATTN_WIDTH, tm), lambda i, j: (i, 0, j)),
            rows(SGU_WIDTH),
            rows(PLE_DIM),
            resident(wo), resident(win), resident(wout), resident(wple), resident(wgate),
            small(g1), small(b1), small(g2), small(b2),
        ],
        out_specs=rows(d),
        out_shape=jax.ShapeDtypeStruct(x.shape, x.dtype),
        scratch_shapes=[pltpu.VMEM((tm, d), jnp.float32)],
        compiler_params=pltpu.CompilerParams(
            dimension_semantics=("parallel", "parallel"),
            vmem_limit_bytes=VMEM_LIMIT),
        name="mixffn",
    )(x, at, bo, p, wo, win, wout, wple, wgate, g1, b1, g2, b2)


def kernel(x, p, positions, w_in, w_o, ln1_g, ln1_b, ln2_g, ln2_b, sgu_w, sgu_b,
           sgu_ln_g, sgu_ln_b, w_ffn_in, w_ffn_out, w_ple, w_ple_gate):
    depth = w_in.shape[0]
    alpha = (2 * depth) ** 0.25
    bf = jnp.bfloat16
    attn_cols = ROW_WI + N_IDX_HEADS

    cos_t, sin_t = _rope_tables(positions)

    for i in range(depth):
        wta = jnp.pad(w_in[i, :, :attn_cols].T.astype(bf),
                      ((0, ATTN_ROWS - attn_cols), (0, 0)))
        wsgu = w_in[i, :, attn_cols:].astype(bf)
        sgu_bias = jnp.repeat(sgu_b[i].T, SGU_GROUP_DIM, axis=1)

        qt, kn, vt, qit, kin, wit, bo = _proj(
            x, wta, wsgu, cos_t, sin_t, sgu_w[i], sgu_bias,
            sgu_ln_g[i][None], sgu_ln_b[i][None])
        at = _attn(qit, wit, qt, kin, kn, vt)
        x = _mixffn(x, at, bo, p[i], w_o[i].astype(bf),
                    w_ffn_in[i].astype(bf), w_ffn_out[i].astype(bf),
                    w_ple[i].astype(bf), w_ple_gate[i].astype(bf),
                    ln1_g[i][None], ln1_b[i][None], ln2_g[i][None], ln2_b[i][None], alpha)
    return x
```

```python
import functools
import math

import jax
import jax.numpy as jnp
from jax import lax
from jax.experimental import pallas as pl
from jax.experimental.pallas import tpu as pltpu

D_MODEL = 1024
CHUNK = 64
N_HEADS = 8
N_KV_HEADS = 2
HEAD_DIM = 64
ATTN_WIDTH = N_HEADS * HEAD_DIM
ROPE_DIM = HEAD_DIM // 4
ROPE_HALF = ROPE_DIM // 2
ROPE_THETA = 500000.0
N_IDX_HEADS = 4
IDX_DIM = 64
TOPK_MAX = 256
SGU_CHUNK = 128
SGU_GROUPS = 8
SGU_WIDTH = D_MODEL - ATTN_WIDTH
SGU_GROUP_DIM = SGU_WIDTH // SGU_GROUPS
D_FF = 2816
PLE_DIM = 256
LN_EPS = 1e-5

Q_COLS = N_HEADS * HEAD_DIM
KV_COLS = N_KV_HEADS * HEAD_DIM
IQ_COLS = N_IDX_HEADS * IDX_DIM

ROW_Q = 0
ROW_K = ROW_Q + Q_COLS
ROW_V = ROW_K + KV_COLS
ROW_QI = ROW_V + KV_COLS
ROW_KI = ROW_QI + IQ_COLS
ROW_WI = ROW_KI + IDX_DIM
ATTN_ROWS = 1152

LANES = 128
SUBLANES = 8
BF16_ROWS = 16
VMEM_LIMIT = 56 * 1024 * 1024

PROJ_TM = 512
FFN_TM = 512
FFN_CHUNK = 256
ATTN_QB = 256
ATTN_KT = 256
ATTN_HEADS_PER_PASS = 2
TILE_UNROLL = (4, 2, 1)
HI_BITS = 16
DIGIT_BITS = 8
KEY_OF_NEG_INF = -0x7F800001
HI_KEY_OF_NEG_INF = -0x7F810000
MIN_NORMAL_KEY = 0x00800000
INT_MIN = -0x80000000
NEG_INF = float("-inf")
LOG2E = math.log2(math.e)


def _layer_norm(y, g, b):
    mu = jnp.mean(y, axis=-1, keepdims=True)
    d = y - mu
    var = jnp.mean(d * d, axis=-1, keepdims=True)
    return d * lax.rsqrt(var + LN_EPS) * g + b


def _dot(a, b):
    return jnp.dot(a, b, preferred_element_type=jnp.float32)


def _dot_nt(a, b):
    return lax.dot_general(a, b, (((1,), (1,)), ((), ())),
                           preferred_element_type=jnp.float32)


def _dot_tn(a, b):
    return lax.dot_general(a, b, (((0,), (0,)), ((), ())),
                           preferred_element_type=jnp.float32)


def _rope_table_kernel(pos_ref, cos_ref, sin_ref):
    seq = pos_ref.shape[-1]
    pos = pos_ref[0].astype(jnp.float32)
    i = lax.broadcasted_iota(jnp.int32, (ROPE_HALF, seq), 0).astype(jnp.float32)
    inv = jnp.power(jnp.float32(ROPE_THETA), i * (-2.0 / ROPE_DIM))
    ang = pos * inv
    cos_ref[0] = jnp.cos(ang)
    sin_ref[0] = jnp.sin(ang)


def _rope_tables(positions):
    b, s = positions.shape
    pos3 = positions.reshape(b, 1, s)
    out = jax.ShapeDtypeStruct((b, ROPE_HALF, s), jnp.float32)
    return pl.pallas_call(
        _rope_table_kernel,
        grid=(b,),
        in_specs=[pl.BlockSpec((1, 1, s), lambda i: (i, 0, 0))],
        out_specs=[pl.BlockSpec((1, ROPE_HALF, s), lambda i: (i, 0, 0))] * 2,
        out_shape=[out, out],
        name="rope_tables",
    )(pos3)


def _rope_rows(ht, cos, sin):
    x1 = ht[0:ROPE_HALF]
    x2 = ht[ROPE_HALF:ROPE_DIM]
    return jnp.concatenate(
        [x1 * cos - x2 * sin, x2 * cos + x1 * sin, ht[ROPE_DIM:]], axis=0)


def _proj_kernel(x_ref, wta_ref, wsgu_ref, cos_ref, sin_ref, sw_ref, sb_ref,
                 lng_ref, lnb_ref,
                 qt_ref, kn_ref, vt_ref, qit_ref, kin_ref, wit_ref, bo_ref):
    tm = x_ref.shape[1]
    xb = x_ref[0].astype(jnp.bfloat16)
    cos = cos_ref[0]
    sin = sin_ref[0]

    def store_query_blocks(ref, rows, val):
        for qb in range(tm // ATTN_QB):
            ref[0, qb, rows, :] = val[:, qb * ATTN_QB:(qb + 1) * ATTN_QB].astype(ref.dtype)

    q_scale = HEAD_DIM ** -0.5 * LOG2E
    qt = _dot_nt(wta_ref[ROW_Q:ROW_K, :], xb)
    for h in range(N_HEADS):
        head = _rope_rows(qt[h * HEAD_DIM:(h + 1) * HEAD_DIM], cos, sin)
        store_query_blocks(qt_ref, slice(h * HEAD_DIM, (h + 1) * HEAD_DIM), head * q_scale)

    kt = _dot_nt(wta_ref[ROW_K:ROW_V, :], xb)
    kt = jnp.concatenate(
        [_rope_rows(kt[g * HEAD_DIM:(g + 1) * HEAD_DIM], cos, sin)
         for g in range(N_KV_HEADS)], axis=0)
    kn = kt.T

    vt = _dot_nt(wta_ref[ROW_V:ROW_QI, :], xb)

    qit = _dot_nt(wta_ref[ROW_QI:ROW_KI, :], xb)
    for h in range(N_IDX_HEADS):
        head = _rope_rows(qit[h * IDX_DIM:(h + 1) * IDX_DIM], cos, sin)
        store_query_blocks(qit_ref, slice(h * IDX_DIM, (h + 1) * IDX_DIM), head)

    kwt = _dot_nt(wta_ref[ROW_KI:ATTN_ROWS, :], xb)
    kit = _rope_rows(kwt[0:IDX_DIM], cos, sin)
    kit = jnp.concatenate([kit, jnp.zeros_like(kit)], axis=0)
    kin = kit.T
    store_query_blocks(wit_ref, slice(0, SUBLANES),
                       kwt[IDX_DIM:IDX_DIM + SUBLANES] * ((N_IDX_HEADS * IDX_DIM) ** -0.5))

    for c in range(tm // ATTN_KT):
        rows = slice(c * ATTN_KT, (c + 1) * ATTN_KT)
        kn_ref[0, c] = kn[rows].astype(kn_ref.dtype)
        kin_ref[0, c] = kin[rows].astype(kin_ref.dtype)
        vt_ref[0, c] = vt[:, rows].astype(vt_ref.dtype)

    uv = _dot(xb, wsgu_ref[...])
    gu = jax.nn.gelu(uv[:, :SGU_WIDTH])
    gv = jax.nn.gelu(uv[:, SGU_WIDTH:])
    gv = _layer_norm(gv, lng_ref[...], lnb_ref[...]).astype(jnp.bfloat16)

    row = lax.broadcasted_iota(jnp.int32, (SGU_CHUNK, SGU_CHUNK), 0)
    col = lax.broadcasted_iota(jnp.int32, (SGU_CHUNK, SGU_CHUNK), 1)
    tri = col <= row
    ws = [jnp.where(tri, sw_ref[g], 0.0).astype(jnp.bfloat16)
          for g in range(SGU_GROUPS)]
    lane = lax.broadcasted_iota(jnp.int32, (SGU_CHUNK, LANES), 1)
    first_group = lane < SGU_GROUP_DIM
    for c in range(tm // SGU_CHUNK):
        rows = slice(c * SGU_CHUNK, (c + 1) * SGU_CHUNK)
        for pr in range(SGU_WIDTH // LANES):
            cols = slice(pr * LANES, (pr + 1) * LANES)
            vblk = gv[rows, cols]
            mixed = jnp.where(first_group,
                              _dot(ws[2 * pr], vblk), _dot(ws[2 * pr + 1], vblk))
            mixed = mixed + sb_ref[:, cols]
            bo_ref[0, rows, cols] = (gu[rows, cols] * mixed).astype(bo_ref.dtype)


def _proj(x, wta, wsgu, cos_t, sin_t, sgu_w, sgu_bias, ln_g, ln_b):
    b, s, d = x.shape
    tm = PROJ_TM
    nt = s // ATTN_KT
    tpt = tm // ATTN_KT
    grid = (b, s // tm)
    const2 = lambda i, j: (0, 0)
    in_specs = [
        pl.BlockSpec((1, tm, d), lambda i, j: (i, j, 0)),
        pl.BlockSpec(wta.shape, const2),
        pl.BlockSpec(wsgu.shape, const2),
        pl.BlockSpec((1, ROPE_HALF, tm), lambda i, j: (i, 0, j)),
        pl.BlockSpec((1, ROPE_HALF, tm), lambda i, j: (i, 0, j)),
        pl.BlockSpec(sgu_w.shape, lambda i, j: (0, 0, 0)),
        pl.BlockSpec(sgu_bias.shape, const2),
        pl.BlockSpec(ln_g.shape, const2),
        pl.BlockSpec(ln_b.shape, const2),
    ]
    key_rows = pl.BlockSpec((1, tpt, ATTN_KT, LANES), lambda i, j: (i, j, 0, 0))
    key_cols = pl.BlockSpec((1, tpt, LANES, ATTN_KT), lambda i, j: (i, j, 0, 0))
    query_blocks = lambda rows: pl.BlockSpec((1, tm // ATTN_QB, rows, ATTN_QB),
                                             lambda i, j: (i, j, 0, 0))
    out_specs = [
        query_blocks(Q_COLS),
        key_rows,
        key_cols,
        query_blocks(IQ_COLS),
        key_rows,
        query_blocks(SUBLANES),
        pl.BlockSpec((1, tm, SGU_WIDTH), lambda i, j: (i, j, 0)),
    ]
    bf = jnp.bfloat16
    nq = s // ATTN_QB
    out_shape = [
        jax.ShapeDtypeStruct((b, nq, Q_COLS, ATTN_QB), bf),
        jax.ShapeDtypeStruct((b, nt, ATTN_KT, LANES), bf),
        jax.ShapeDtypeStruct((b, nt, LANES, ATTN_KT), bf),
        jax.ShapeDtypeStruct((b, nq, IQ_COLS, ATTN_QB), bf),
        jax.ShapeDtypeStruct((b, nt, ATTN_KT, LANES), bf),
        jax.ShapeDtypeStruct((b, nq, SUBLANES, ATTN_QB), jnp.float32),
        jax.ShapeDtypeStruct((b, s, SGU_WIDTH), bf),
    ]
    return pl.pallas_call(
        _proj_kernel,
        grid=grid,
        in_specs=in_specs,
        out_specs=out_specs,
        out_shape=out_shape,
        compiler_params=pltpu.CompilerParams(
            dimension_semantics=("parallel", "parallel"),
            vmem_limit_bytes=VMEM_LIMIT),
        name="proj",
    )(x, wta, wsgu, cos_t, sin_t, sgu_w, sgu_bias, ln_g, ln_b)


def _fold_sublanes(x, op):
    return op(x.reshape(x.shape[0] // SUBLANES, SUBLANES, x.shape[1]), axis=0)


def _tile_loop(lo, hi, body, init):
    carry, start = init, lo
    for width in TILE_UNROLL:
        def group(i, c, width=width, start=start):
            for u in range(width):
                c = body(start + i * width + u, c)
            return c
        n_groups = (hi - start) // width
        carry = lax.fori_loop(0, n_groups, group, carry)
        start = start + n_groups * width
    return carry


def _count_bf16(ref, n_tiles, pred):
    one = jnp.ones((), jnp.bfloat16)
    zero = jnp.zeros((), jnp.bfloat16)
    groups = ATTN_KT // BF16_ROWS
    n_acc = 4
    assert groups * n_tiles <= 256 * n_acc
    accs = [None] * n_acc
    for c in range(n_tiles):
        m3 = jnp.where(pred(ref[c]), one, zero).reshape(groups, BF16_ROWS, ATTN_QB)
        for i in range(groups):
            k = i % n_acc
            accs[k] = m3[i] if accs[k] is None else accs[k] + m3[i]
    acc = (accs[0] + accs[1]) + (accs[2] + accs[3])
    return jnp.sum(acc.astype(jnp.float32), axis=0, keepdims=True)


def _bit_search(n_bits, count_ge, target):
    def body(i, val):
        trial = val | jnp.left_shift(jnp.int32(1), (n_bits - 1) - i)
        return jnp.where(count_ge(trial) >= target, trial, val)
    return lax.fori_loop(0, n_bits, body, jnp.zeros((1, ATTN_QB), jnp.int32))


def _hi_to_bf16(uhi):
    key = jnp.maximum(jnp.left_shift(uhi, HI_BITS) ^ jnp.int32(INT_MIN), HI_KEY_OF_NEG_INF)
    key = jnp.where(key > 0, jnp.maximum(key, MIN_NORMAL_KEY), key)
    bits = key ^ ((key >> 31) & jnp.int32(0x7FFF0000))
    return pltpu.bitcast(bits, jnp.float32).astype(jnp.bfloat16)


def _digit_to_bf16(d):
    return d.astype(jnp.float32).astype(jnp.bfloat16)


def _threshold_search(n_tiles, key_ref, hb_ref, dg_ref, thr_ref, room_ref):
    digit_mask = (1 << DIGIT_BITS) - 1
    minus1 = jnp.full((), -1, jnp.bfloat16)
    target = jnp.float32(TOPK_MAX)

    def count_ge(ref, cand):
        return _count_bf16(ref, n_tiles, lambda blk: blk >= cand)

    uhi = _bit_search(HI_BITS, lambda t: count_ge(hb_ref, _hi_to_bf16(t)), target)
    hi_f = _hi_to_bf16(uhi)
    target = target - _count_bf16(hb_ref, n_tiles, lambda blk: blk > hi_f)

    for c in range(n_tiles):
        dig = _digit_to_bf16((key_ref[c] >> DIGIT_BITS) & digit_mask)
        dg_ref[c] = jnp.where(hb_ref[c] == hi_f, dig, minus1)
    d1 = _bit_search(DIGIT_BITS, lambda t: count_ge(dg_ref, _digit_to_bf16(t)), target)
    d1_f = _digit_to_bf16(d1)
    target = target - _count_bf16(dg_ref, n_tiles, lambda blk: blk > d1_f)

    for c in range(n_tiles):
        dig = _digit_to_bf16(key_ref[c] & digit_mask)
        hb_ref[c] = jnp.where(dg_ref[c] == d1_f, dig, minus1)
    d0 = _bit_search(DIGIT_BITS, lambda t: count_ge(hb_ref, _digit_to_bf16(t)), target)
    d0_f = _digit_to_bf16(d0)
    room = target - _count_bf16(hb_ref, n_tiles, lambda blk: blk > d0_f)

    ukey = jnp.left_shift(uhi, HI_BITS) | jnp.left_shift(d1, DIGIT_BITS) | d0
    thr = jnp.maximum(ukey ^ jnp.int32(INT_MIN), KEY_OF_NEG_INF + 1)
    thr_ref[...] = jnp.broadcast_to(thr, thr_ref.shape)
    room_ref[...] = jnp.broadcast_to(room, room_ref.shape)


def _attn_kernel(qit_ref, wit_ref, qt_ref, kin_ref, kn_ref, vt_ref, at_ref,
                 key_ref, hb_ref, dg_ref, bias_ref, s_ref, thr_ref, room_ref,
                 qpad_ref, o_ref):
    j = pl.program_id(1)
    nt = j + 1
    max_tiles = key_ref.shape[0]

    zeros_i = jnp.zeros((LANES - IDX_DIM, ATTN_QB), jnp.bfloat16)
    qi_pad = [jnp.concatenate([qit_ref[0, 0, h * IDX_DIM:(h + 1) * IDX_DIM, :], zeros_i], axis=0)
              for h in range(N_IDX_HEADS)]
    wit = wit_ref[0, 0]

    def tile_scores(c):
        kt = kin_ref[0, c]
        sc = jnp.zeros((ATTN_KT, ATTN_QB), jnp.float32)
        for h in range(N_IDX_HEADS):
            sc = sc + jnp.maximum(_dot(kt, qi_pad[h]), 0.0) * wit[h:h + 1, :]
        return sc

    def store_scores(c, sc):
        bits = pltpu.bitcast(sc, jnp.int32)
        bits = jnp.where((bits & jnp.int32(0x7F800000)) == 0, 0, bits)
        key_ref[c] = bits ^ ((bits >> 31) & jnp.int32(0x7FFFFFFF))
        hb_ref[c] = pltpu.bitcast(bits & jnp.int32(-0x10000), jnp.float32).astype(jnp.bfloat16)

    def score_body(c, carry):
        store_scores(c, tile_scores(c))
        return carry

    _tile_loop(0, j, score_body, 0)
    kchunk = lax.broadcasted_iota(jnp.int32, (ATTN_KT, ATTN_QB), 0) // CHUNK
    qchunk = lax.broadcasted_iota(jnp.int32, (ATTN_KT, ATTN_QB), 1) // CHUNK
    store_scores(j, jnp.where(kchunk > qchunk, NEG_INF, tile_scores(j)))

    @pl.when(nt * ATTN_KT <= TOPK_MAX)
    def _():
        def body(c, carry):
            bias_ref[c] = jnp.where(key_ref[c] == KEY_OF_NEG_INF, NEG_INF, 0.0)
            return carry
        lax.fori_loop(0, nt, body, 0)

    for n_tiles in range(TOPK_MAX // ATTN_KT + 1, max_tiles + 1):
        pl.when(nt == n_tiles)(functools.partial(
            _threshold_search, n_tiles, key_ref, hb_ref, dg_ref, thr_ref, room_ref))

    @pl.when(nt * ATTN_KT > TOPK_MAX)
    def _():
        thr = thr_ref[0:1, :]
        room = room_ref[0:1, :]
        krow = lax.broadcasted_iota(jnp.int32, (ATTN_KT, ATTN_KT), 0)
        kcol = lax.broadcasted_iota(jnp.int32, (ATTN_KT, ATTN_KT), 1)
        incl = (kcol <= krow).astype(jnp.bfloat16)

        def tie_body(c, seen):
            key = key_ref[c]
            tie = key == thr
            rank = _dot(incl, jnp.where(tie, 1.0, 0.0).astype(jnp.bfloat16)) + seen
            keep_tie = jnp.where(tie, jnp.where(rank <= room, 0.0, NEG_INF), NEG_INF)
            bias_ref[c] = jnp.where(key > thr, 0.0, keep_tie)
            return rank[ATTN_KT - 1:ATTN_KT, :]

        _tile_loop(0, nt, tie_body, jnp.zeros((1, ATTN_QB), jnp.float32))

    rep = N_HEADS // N_KV_HEADS
    hpp = ATTN_HEADS_PER_PASS
    width = hpp * ATTN_QB
    n_pass = N_HEADS // hpp
    pass_cols = [slice(ps * width, (ps + 1) * width) for ps in range(n_pass)]
    pass_group = [(ps * hpp) // rep for ps in range(n_pass)]

    qpad_ref[...] = jnp.zeros(qpad_ref.shape, qpad_ref.dtype)
    for h in range(N_HEADS):
        g = h // rep
        qpad_ref[g * HEAD_DIM:(g + 1) * HEAD_DIM, h * ATTN_QB:(h + 1) * ATTN_QB] = (
            qt_ref[0, 0, h * HEAD_DIM:(h + 1) * HEAD_DIM, :])

    def qk_body(c, m_accs):
        kt = kn_ref[0, c]
        bias = jnp.concatenate([bias_ref[c]] * hpp, axis=1)
        out = []
        for ps in range(n_pass):
            s = _dot(kt, qpad_ref[:, pass_cols[ps]]) + bias
            s_ref[c, :, pass_cols[ps]] = s
            out.append(jnp.maximum(m_accs[ps], _fold_sublanes(s, jnp.max)))
        return tuple(out)

    m_accs = _tile_loop(
        0, nt, qk_body,
        tuple(jnp.full((SUBLANES, width), NEG_INF, jnp.float32) for _ in range(n_pass)))
    m_rows = [jnp.max(m, axis=0, keepdims=True) for m in m_accs]

    o_ref[...] = jnp.zeros(o_ref.shape, o_ref.dtype)

    def pv_body(c, l_accs):
        out = []
        for ps in range(n_pass):
            g = pass_group[ps]
            p = jnp.exp2(s_ref[c, :, pass_cols[ps]] - m_rows[ps])
            out.append(l_accs[ps] + _fold_sublanes(p, jnp.sum))
            v_blk = vt_ref[0, c, g * HEAD_DIM:(g + 1) * HEAD_DIM, :]
            o_ref[ps] += _dot(v_blk, p.astype(jnp.bfloat16))
        return tuple(out)

    l_accs = _tile_loop(
        0, nt, pv_body,
        tuple(jnp.zeros((SUBLANES, width), jnp.float32) for _ in range(n_pass)))
    for ps in range(n_pass):
        out = o_ref[ps] / jnp.sum(l_accs[ps], axis=0, keepdims=True)
        for r in range(hpp):
            h = ps * hpp + r
            at_ref[0, 0, h * HEAD_DIM:(h + 1) * HEAD_DIM, :] = (
                out[:, r * ATTN_QB:(r + 1) * ATTN_QB].astype(at_ref.dtype))


def _attn(qit, wit, qt, kin, kn, vt):
    b, nq = qt.shape[0], qt.shape[1]
    nt = kn.shape[1]
    per_q = lambda rows: pl.BlockSpec((1, 1, rows, ATTN_QB), lambda i, j: (i, j, 0, 0))
    per_b = lambda arr: pl.BlockSpec((1,) + arr.shape[1:], lambda i, j: (i, 0, 0, 0))
    tile = (nt, ATTN_KT, ATTN_QB)
    return pl.pallas_call(
        _attn_kernel,
        grid=(b, nq),
        in_specs=[per_q(IQ_COLS), per_q(SUBLANES), per_q(Q_COLS),
                  per_b(kin), per_b(kn), per_b(vt)],
        out_specs=per_q(ATTN_WIDTH),
        out_shape=jax.ShapeDtypeStruct((b, nq, ATTN_WIDTH, ATTN_QB), jnp.bfloat16),
        scratch_shapes=[
            pltpu.VMEM(tile, jnp.int32),
            pltpu.VMEM(tile, jnp.bfloat16),
            pltpu.VMEM(tile, jnp.bfloat16),
            pltpu.VMEM(tile, jnp.float32),
            pltpu.VMEM((nt, ATTN_KT, N_HEADS * ATTN_QB), jnp.float32),
            pltpu.VMEM((SUBLANES, ATTN_QB), jnp.int32),
            pltpu.VMEM((SUBLANES, ATTN_QB), jnp.float32),
            pltpu.VMEM((N_KV_HEADS * HEAD_DIM, N_HEADS * ATTN_QB), jnp.bfloat16),
            pltpu.VMEM((N_HEADS // ATTN_HEADS_PER_PASS, HEAD_DIM,
                        ATTN_HEADS_PER_PASS * ATTN_QB), jnp.float32),
        ],
        compiler_params=pltpu.CompilerParams(
            dimension_semantics=("parallel", "arbitrary"),
            vmem_limit_bytes=VMEM_LIMIT),
        name="attn",
    )(qit, wit, qt, kin, kn, vt)


def _mixffn_kernel(alpha, x_ref, at_ref, bo_ref, p_ref, wo_ref, win_ref, wout_ref,
                   wple_ref, wgate_ref, g1_ref, b1_ref, g2_ref, b2_ref, o_ref, acc_ref):
    a_mix = jnp.concatenate([_dot_tn(at_ref[0, qb], wo_ref[0:ATTN_WIDTH, :])
                             for qb in range(at_ref.shape[1])], axis=0)
    mix = a_mix + _dot(bo_ref[0], wo_ref[ATTN_WIDTH:, :])
    x = _layer_norm(alpha * x_ref[0] + mix, g1_ref[...], b1_ref[...])
    xb = x.astype(jnp.bfloat16)
    gate_lin = _dot(xb, wgate_ref[...])
    ple = _dot(p_ref[0].astype(jnp.bfloat16), wple_ref[...]) * jax.nn.sigmoid(gate_lin)
    acc_ref[...] = alpha * x + ple
    for c in range(D_FF // FFN_CHUNK):
        cols = slice(c * FFN_CHUNK, (c + 1) * FFN_CHUNK)
        up_cols = slice(D_FF + c * FFN_CHUNK, D_FF + (c + 1) * FFN_CHUNK)
        gate = _dot(xb, win_ref[:, cols])
        up = _dot(xb, win_ref[:, up_cols])
        h = (jax.nn.silu(gate) * up).astype(jnp.bfloat16)
        acc_ref[...] += _dot(h, wout_ref[cols, :])
    o_ref[0] = _layer_norm(acc_ref[...], g2_ref[...], b2_ref[...])


def _mixffn(x, at, bo, p, wo, win, wout, wple, wgate, g1, b1, g2, b2, alpha):
    b, s, d = x.shape
    tm = FFN_TM
    const2 = lambda i, j: (0, 0)
    resident = lambda w: pl.BlockSpec(w.shape, const2, pipeline_mode=pl.Buffered(1))
    rows = lambda width: pl.BlockSpec((1, tm, width), lambda i, j: (i, j, 0))
    small = lambda v: pl.BlockSpec(v.shape, const2)
    return pl.pallas_call(
        functools.partial(_mixffn_kernel, alpha),
        grid=(b, s // tm),
        in_specs=[
            rows(d),
            pl.BlockSpec((1, tm // ATTN_QB, ATTN_WIDTH, ATTN_QB), lambda i, j: (i, j, 0, 0)),
            rows(SGU_WIDTH),
            rows(PLE_DIM),
            resident(wo), resident(win), resident(wout), resident(wple), resident(wgate),
            small(g1), small(b1), small(g2), small(b2),
        ],
        out_specs=rows(d),
        out_shape=jax.ShapeDtypeStruct(x.shape, x.dtype),
        scratch_shapes=[pltpu.VMEM((tm, d), jnp.float32)],
        compiler_params=pltpu.CompilerParams(
            dimension_semantics=("parallel", "parallel"),
            vmem_limit_bytes=VMEM_LIMIT),
        name="mixffn",
    )(x, at, bo, p, wo, win, wout, wple, wgate, g1, b1, g2, b2)


def kernel(x, p, positions, w_in, w_o, ln1_g, ln1_b, ln2_g, ln2_b, sgu_w, sgu_b,
           sgu_ln_g, sgu_ln_b, w_ffn_in, w_ffn_out, w_ple, w_ple_gate):
    depth = w_in.shape[0]
    alpha = (2 * depth) ** 0.25
    bf = jnp.bfloat16
    attn_cols = ROW_WI + N_IDX_HEADS

    cos_t, sin_t = _rope_tables(positions)

    for i in range(depth):
        wta = jnp.pad(w_in[i, :, :attn_cols].T.astype(bf),
                      ((0, ATTN_ROWS - attn_cols), (0, 0)))
        wsgu = w_in[i, :, attn_cols:].astype(bf)
        sgu_bias = jnp.repeat(sgu_b[i].T, SGU_GROUP_DIM, axis=1)

        qt, kn, vt, qit, kin, wit, bo = _proj(
            x, wta, wsgu, cos_t, sin_t, sgu_w[i], sgu_bias,
            sgu_ln_g[i][None], sgu_ln_b[i][None])
        at = _attn(qit, wit, qt, kin, kn, vt)
        x = _mixffn(x, at, bo, p[i], w_o[i].astype(bf),
                    w_ffn_in[i].astype(bf), w_ffn_out[i].astype(bf),
                    w_ple[i].astype(bf), w_ple_gate[i].astype(bf),
                    ln1_g[i][None], ln1_b[i][None], ln2_g[i][None], ln2_b[i][None], alpha)
    return x
```

```python
import functools
import math

import jax
import jax.numpy as jnp
from jax import lax
from jax.experimental import pallas as pl
from jax.experimental.pallas import tpu as pltpu

D_MODEL = 1024
CHUNK = 64
N_HEADS = 8
N_KV_HEADS = 2
HEAD_DIM = 64
ATTN_WIDTH = N_HEADS * HEAD_DIM
ROPE_DIM = HEAD_DIM // 4
ROPE_HALF = ROPE_DIM // 2
ROPE_THETA = 500000.0
N_IDX_HEADS = 4
IDX_DIM = 64
TOPK_MAX = 256
SGU_CHUNK = 128
SGU_GROUPS = 8
SGU_WIDTH = D_MODEL - ATTN_WIDTH
SGU_GROUP_DIM = SGU_WIDTH // SGU_GROUPS
D_FF = 2816
PLE_DIM = 256
LN_EPS = 1e-5

Q_COLS = N_HEADS * HEAD_DIM
KV_COLS = N_KV_HEADS * HEAD_DIM
IQ_COLS = N_IDX_HEADS * IDX_DIM

ROW_Q = 0
ROW_K = ROW_Q + Q_COLS
ROW_V = ROW_K + KV_COLS
ROW_QI = ROW_V + KV_COLS
ROW_KI = ROW_QI + IQ_COLS
ROW_WI = ROW_KI + IDX_DIM
ATTN_ROWS = 1152

LANES = 128
SUBLANES = 8
BF16_ROWS = 16
VMEM_LIMIT = 56 * 1024 * 1024

PROJ_TM = 512
FFN_TM = 512
FFN_CHUNK = 256
ATTN_QB = 256
ATTN_KT = 256
ATTN_HEADS_PER_PASS = 2
TILE_UNROLL = (4, 2, 1)
HI_BITS = 16
REFINE_BITS = 17
KEY_OF_NEG_INF = -0x7F800001
HI_KEY_OF_NEG_INF = -0x7F810000
MIN_NORMAL_KEY = 0x00800000
INT_MIN = -0x80000000
NEG_INF = float("-inf")
LOG2E = math.log2(math.e)


def _layer_norm(y, g, b):
    mu = jnp.mean(y, axis=-1, keepdims=True)
    d = y - mu
    var = jnp.mean(d * d, axis=-1, keepdims=True)
    return d * lax.rsqrt(var + LN_EPS) * g + b


def _dot(a, b):
    return jnp.dot(a, b, preferred_element_type=jnp.float32)


def _dot_nt(a, b):
    return lax.dot_general(a, b, (((1,), (1,)), ((), ())),
                           preferred_element_type=jnp.float32)


def _dot_tn(a, b):
    return lax.dot_general(a, b, (((0,), (0,)), ((), ())),
                           preferred_element_type=jnp.float32)


def _rope_table_kernel(pos_ref, cos_ref, sin_ref):
    seq = pos_ref.shape[-1]
    pos = pos_ref[0].astype(jnp.float32)
    i = lax.broadcasted_iota(jnp.int32, (ROPE_HALF, seq), 0).astype(jnp.float32)
    inv = jnp.power(jnp.float32(ROPE_THETA), i * (-2.0 / ROPE_DIM))
    ang = pos * inv
    cos_ref[0] = jnp.cos(ang)
    sin_ref[0] = jnp.sin(ang)


def _rope_tables(positions):
    b, s = positions.shape
    pos3 = positions.reshape(b, 1, s)
    out = jax.ShapeDtypeStruct((b, ROPE_HALF, s), jnp.float32)
    return pl.pallas_call(
        _rope_table_kernel,
        grid=(b,),
        in_specs=[pl.BlockSpec((1, 1, s), lambda i: (i, 0, 0))],
        out_specs=[pl.BlockSpec((1, ROPE_HALF, s), lambda i: (i, 0, 0))] * 2,
        out_shape=[out, out],
        name="rope_tables",
    )(pos3)


def _rope_rows(ht, cos, sin):
    x1 = ht[0:ROPE_HALF]
    x2 = ht[ROPE_HALF:ROPE_DIM]
    return jnp.concatenate(
        [x1 * cos - x2 * sin, x2 * cos + x1 * sin, ht[ROPE_DIM:]], axis=0)


def _proj_kernel(x_ref, wta_ref, wsgu_ref, cos_ref, sin_ref, sw_ref, sb_ref,
                 lng_ref, lnb_ref,
                 qt_ref, kn_ref, vt_ref, qit_ref, kin_ref, wit_ref, bo_ref):
    tm = x_ref.shape[1]
    xb = x_ref[0].astype(jnp.bfloat16)
    cos = cos_ref[0]
    sin = sin_ref[0]

    def store_query_blocks(ref, rows, val):
        for qb in range(tm // ATTN_QB):
            ref[0, qb, rows, :] = val[:, qb * ATTN_QB:(qb + 1) * ATTN_QB].astype(ref.dtype)

    q_scale = HEAD_DIM ** -0.5 * LOG2E
    qt = _dot_nt(wta_ref[ROW_Q:ROW_K, :], xb)
    for h in range(N_HEADS):
        head = _rope_rows(qt[h * HEAD_DIM:(h + 1) * HEAD_DIM], cos, sin)
        store_query_blocks(qt_ref, slice(h * HEAD_DIM, (h + 1) * HEAD_DIM), head * q_scale)

    kt = _dot_nt(wta_ref[ROW_K:ROW_V, :], xb)
    kt = jnp.concatenate(
        [_rope_rows(kt[g * HEAD_DIM:(g + 1) * HEAD_DIM], cos, sin)
         for g in range(N_KV_HEADS)], axis=0)
    kn = kt.T

    vt = _dot_nt(wta_ref[ROW_V:ROW_QI, :], xb)

    qit = _dot_nt(wta_ref[ROW_QI:ROW_KI, :], xb)
    for h in range(N_IDX_HEADS):
        head = _rope_rows(qit[h * IDX_DIM:(h + 1) * IDX_DIM], cos, sin)
        store_query_blocks(qit_ref, slice(h * IDX_DIM, (h + 1) * IDX_DIM), head)

    kwt = _dot_nt(wta_ref[ROW_KI:ATTN_ROWS, :], xb)
    kit = _rope_rows(kwt[0:IDX_DIM], cos, sin)
    kit = jnp.concatenate([kit, jnp.zeros_like(kit)], axis=0)
    kin = kit.T
    store_query_blocks(wit_ref, slice(0, SUBLANES),
                       kwt[IDX_DIM:IDX_DIM + SUBLANES] * ((N_IDX_HEADS * IDX_DIM) ** -0.5))

    for c in range(tm // ATTN_KT):
        rows = slice(c * ATTN_KT, (c + 1) * ATTN_KT)
        kn_ref[0, c] = kn[rows].astype(kn_ref.dtype)
        kin_ref[0, c] = kin[rows].astype(kin_ref.dtype)
        vt_ref[0, c] = vt[:, rows].astype(vt_ref.dtype)

    uv = _dot(xb, wsgu_ref[...])
    gu = jax.nn.gelu(uv[:, :SGU_WIDTH])
    gv = jax.nn.gelu(uv[:, SGU_WIDTH:])
    gv = _layer_norm(gv, lng_ref[...], lnb_ref[...]).astype(jnp.bfloat16)

    row = lax.broadcasted_iota(jnp.int32, (SGU_CHUNK, SGU_CHUNK), 0)
    col = lax.broadcasted_iota(jnp.int32, (SGU_CHUNK, SGU_CHUNK), 1)
    tri = col <= row
    ws = [jnp.where(tri, sw_ref[g], 0.0).astype(jnp.bfloat16)
          for g in range(SGU_GROUPS)]
    lane = lax.broadcasted_iota(jnp.int32, (SGU_CHUNK, LANES), 1)
    first_group = lane < SGU_GROUP_DIM
    for c in range(tm // SGU_CHUNK):
        rows = slice(c * SGU_CHUNK, (c + 1) * SGU_CHUNK)
        for pr in range(SGU_WIDTH // LANES):
            cols = slice(pr * LANES, (pr + 1) * LANES)
            vblk = gv[rows, cols]
            mixed = jnp.where(first_group,
                              _dot(ws[2 * pr], vblk), _dot(ws[2 * pr + 1], vblk))
            mixed = mixed + sb_ref[:, cols]
            bo_ref[0, rows, cols] = (gu[rows, cols] * mixed).astype(bo_ref.dtype)


def _proj(x, wta, wsgu, cos_t, sin_t, sgu_w, sgu_bias, ln_g, ln_b):
    b, s, d = x.shape
    tm = PROJ_TM
    nt = s // ATTN_KT
    tpt = tm // ATTN_KT
    grid = (b, s // tm)
    const2 = lambda i, j: (0, 0)
    in_specs = [
        pl.BlockSpec((1, tm, d), lambda i, j: (i, j, 0)),
        pl.BlockSpec(wta.shape, const2),
        pl.BlockSpec(wsgu.shape, const2),
        pl.BlockSpec((1, ROPE_HALF, tm), lambda i, j: (i, 0, j)),
        pl.BlockSpec((1, ROPE_HALF, tm), lambda i, j: (i, 0, j)),
        pl.BlockSpec(sgu_w.shape, lambda i, j: (0, 0, 0)),
        pl.BlockSpec(sgu_bias.shape, const2),
        pl.BlockSpec(ln_g.shape, const2),
        pl.BlockSpec(ln_b.shape, const2),
    ]
    key_rows = pl.BlockSpec((1, tpt, ATTN_KT, LANES), lambda i, j: (i, j, 0, 0))
    key_cols = pl.BlockSpec((1, tpt, LANES, ATTN_KT), lambda i, j: (i, j, 0, 0))
    query_blocks = lambda rows: pl.BlockSpec((1, tm // ATTN_QB, rows, ATTN_QB),
                                             lambda i, j: (i, j, 0, 0))
    out_specs = [
        query_blocks(Q_COLS),
        key_rows,
        key_cols,
        query_blocks(IQ_COLS),
        key_rows,
        query_blocks(SUBLANES),
        pl.BlockSpec((1, tm, SGU_WIDTH), lambda i, j: (i, j, 0)),
    ]
    bf = jnp.bfloat16
    nq = s // ATTN_QB
    out_shape = [
        jax.ShapeDtypeStruct((b, nq, Q_COLS, ATTN_QB), bf),
        jax.ShapeDtypeStruct((b, nt, ATTN_KT, LANES), bf),
        jax.ShapeDtypeStruct((b, nt, LANES, ATTN_KT), bf),
        jax.ShapeDtypeStruct((b, nq, IQ_COLS, ATTN_QB), bf),
        jax.ShapeDtypeStruct((b, nt, ATTN_KT, LANES), bf),
        jax.ShapeDtypeStruct((b, nq, SUBLANES, ATTN_QB), jnp.float32),
        jax.ShapeDtypeStruct((b, s, SGU_WIDTH), bf),
    ]
    return pl.pallas_call(
        _proj_kernel,
        grid=grid,
        in_specs=in_specs,
        out_specs=out_specs,
        out_shape=out_shape,
        compiler_params=pltpu.CompilerParams(
            dimension_semantics=("parallel", "parallel"),
            vmem_limit_bytes=VMEM_LIMIT),
        name="proj",
    )(x, wta, wsgu, cos_t, sin_t, sgu_w, sgu_bias, ln_g, ln_b)


def _fold_sublanes(x, op):
    return op(x.reshape(x.shape[0] // SUBLANES, SUBLANES, x.shape[1]), axis=0)


def _tile_loop(lo, hi, body, init):
    carry, start = init, lo
    for width in TILE_UNROLL:
        def group(i, c, width=width, start=start):
            for u in range(width):
                c = body(start + i * width + u, c)
            return c
        n_groups = (hi - start) // width
        carry = lax.fori_loop(0, n_groups, group, carry)
        start = start + n_groups * width
    return carry


def _count_bf16(ref, n_tiles, pred):
    one = jnp.ones((), jnp.bfloat16)
    zero = jnp.zeros((), jnp.bfloat16)
    groups = ATTN_KT // BF16_ROWS
    n_acc = 4
    assert groups * n_tiles <= 256 * n_acc
    accs = [None] * n_acc
    for c in range(n_tiles):
        m3 = jnp.where(pred(ref[c]), one, zero).reshape(groups, BF16_ROWS, ATTN_QB)
        for i in range(groups):
            k = i % n_acc
            accs[k] = m3[i] if accs[k] is None else accs[k] + m3[i]
    acc = (accs[0] + accs[1]) + (accs[2] + accs[3])
    return jnp.sum(acc.astype(jnp.float32), axis=0, keepdims=True)


def _bit_search(n_bits, count_ge, target):
    def body(i, val):
        trial = val | jnp.left_shift(jnp.int32(1), (n_bits - 1) - i)
        return jnp.where(count_ge(trial) >= target, trial, val)
    return lax.fori_loop(0, n_bits, body, jnp.zeros((1, ATTN_QB), jnp.int32))


def _key_to_float(key):
    bits = key ^ ((key >> 31) & jnp.int32(0x7FFFFFFF))
    return pltpu.bitcast(bits, jnp.float32)


def _hi_bits(uhi):
    key = jnp.maximum(jnp.left_shift(uhi, HI_BITS) ^ jnp.int32(INT_MIN), HI_KEY_OF_NEG_INF)
    key = jnp.where(key > 0, jnp.maximum(key, MIN_NORMAL_KEY), key)
    return key ^ ((key >> 31) & jnp.int32(0x7FFF0000))


def _hi_to_bf16(uhi):
    return pltpu.bitcast(_hi_bits(uhi), jnp.float32).astype(jnp.bfloat16)


def _count_f32(ref, n_tiles, pred):
    parts = [_fold_sublanes(jnp.where(pred(ref[c]), 1.0, 0.0), jnp.sum)
             for c in range(n_tiles)]
    return jnp.sum(functools.reduce(lambda a, b: a + b, parts), axis=0, keepdims=True)


def _threshold_search(n_tiles, score_ref, hb_ref, thr_ref, room_ref):
    target = jnp.float32(TOPK_MAX)

    def coarse_count(uhi):
        cand = _hi_to_bf16(uhi)
        return _count_bf16(hb_ref, n_tiles, lambda blk: blk >= cand)

    uhi = _bit_search(HI_BITS, coarse_count, target)
    bits = _hi_bits(uhi)
    base = (bits ^ ((bits >> 31) & jnp.int32(0x7FFFFFFF))) - (1 << (HI_BITS - 1))

    def fine_count(offset):
        cand = _key_to_float(jnp.maximum(base + offset, KEY_OF_NEG_INF))
        return _count_f32(score_ref, n_tiles, lambda blk: blk >= cand)

    offset = _bit_search(REFINE_BITS, fine_count, target)
    thr = _key_to_float(jnp.maximum(base + offset, KEY_OF_NEG_INF + 1))
    room = target - _count_f32(score_ref, n_tiles, lambda blk: blk > thr)
    thr_ref[...] = jnp.broadcast_to(thr, thr_ref.shape)
    room_ref[...] = jnp.broadcast_to(room, room_ref.shape)


def _attn_kernel(qit_ref, wit_ref, qt_ref, kin_ref, kn_ref, vt_ref, at_ref,
                 score_ref, hb_ref, bias_ref, s_ref, thr_ref, room_ref,
                 qpad_ref, o_ref):
    j = pl.program_id(1)
    nt = j + 1
    max_tiles = score_ref.shape[0]

    zeros_i = jnp.zeros((LANES - IDX_DIM, ATTN_QB), jnp.bfloat16)
    qi_pad = [jnp.concatenate([qit_ref[0, 0, h * IDX_DIM:(h + 1) * IDX_DIM, :], zeros_i], axis=0)
              for h in range(N_IDX_HEADS)]
    wit = wit_ref[0, 0]

    def tile_scores(c):
        kt = kin_ref[0, c]
        sc = jnp.zeros((ATTN_KT, ATTN_QB), jnp.float32)
        for h in range(N_IDX_HEADS):
            sc = sc + jnp.maximum(_dot(kt, qi_pad[h]), 0.0) * wit[h:h + 1, :]
        return sc

    def store_scores(c, sc):
        score_ref[c] = sc
        hb_ref[c] = sc.astype(jnp.bfloat16)

    def score_body(c, carry):
        store_scores(c, tile_scores(c))
        return carry

    _tile_loop(0, j, score_body, 0)
    kchunk = lax.broadcasted_iota(jnp.int32, (ATTN_KT, ATTN_QB), 0) // CHUNK
    qchunk = lax.broadcasted_iota(jnp.int32, (ATTN_KT, ATTN_QB), 1) // CHUNK
    store_scores(j, jnp.where(kchunk > qchunk, NEG_INF, tile_scores(j)))

    @pl.when(nt * ATTN_KT <= TOPK_MAX)
    def _():
        def body(c, carry):
            bias_ref[c] = jnp.where(score_ref[c] == NEG_INF, NEG_INF, 0.0)
            return carry
        lax.fori_loop(0, nt, body, 0)

    for n_tiles in range(TOPK_MAX // ATTN_KT + 1, max_tiles + 1):
        pl.when(nt == n_tiles)(functools.partial(
            _threshold_search, n_tiles, score_ref, hb_ref, thr_ref, room_ref))

    @pl.when(nt * ATTN_KT > TOPK_MAX)
    def _():
        thr = thr_ref[0:1, :]
        room = room_ref[0:1, :]
        krow = lax.broadcasted_iota(jnp.int32, (ATTN_KT, ATTN_KT), 0)
        kcol = lax.broadcasted_iota(jnp.int32, (ATTN_KT, ATTN_KT), 1)
        incl = (kcol <= krow).astype(jnp.bfloat16)

        def tie_body(c, seen):
            sc = score_ref[c]
            tie = sc == thr
            rank = _dot(incl, jnp.where(tie, 1.0, 0.0).astype(jnp.bfloat16)) + seen
            keep_tie = jnp.where(tie, jnp.where(rank <= room, 0.0, NEG_INF), NEG_INF)
            bias_ref[c] = jnp.where(sc > thr, 0.0, keep_tie)
            return rank[ATTN_KT - 1:ATTN_KT, :]

        _tile_loop(0, nt, tie_body, jnp.zeros((1, ATTN_QB), jnp.float32))

    rep = N_HEADS // N_KV_HEADS
    hpp = ATTN_HEADS_PER_PASS
    width = hpp * ATTN_QB
    n_pass = N_HEADS // hpp
    pass_cols = [slice(ps * width, (ps + 1) * width) for ps in range(n_pass)]
    pass_group = [(ps * hpp) // rep for ps in range(n_pass)]

    qpad_ref[...] = jnp.zeros(qpad_ref.shape, qpad_ref.dtype)
    for h in range(N_HEADS):
        g = h // rep
        qpad_ref[g * HEAD_DIM:(g + 1) * HEAD_DIM, h * ATTN_QB:(h + 1) * ATTN_QB] = (
            qt_ref[0, 0, h * HEAD_DIM:(h + 1) * HEAD_DIM, :])

    def qk_body(c, m_accs):
        kt = kn_ref[0, c]
        bias = jnp.concatenate([bias_ref[c]] * hpp, axis=1)
        out = []
        for ps in range(n_pass):
            s = _dot(kt, qpad_ref[:, pass_cols[ps]]) + bias
            s_ref[c, :, pass_cols[ps]] = s
            out.append(jnp.maximum(m_accs[ps], _fold_sublanes(s, jnp.max)))
        return tuple(out)

    m_accs = _tile_loop(
        0, nt, qk_body,
        tuple(jnp.full((SUBLANES, width), NEG_INF, jnp.float32) for _ in range(n_pass)))
    m_rows = [jnp.max(m, axis=0, keepdims=True) for m in m_accs]

    o_ref[...] = jnp.zeros(o_ref.shape, o_ref.dtype)

    def pv_body(c, l_accs):
        out = []
        for ps in range(n_pass):
            g = pass_group[ps]
            p = jnp.exp2(s_ref[c, :, pass_cols[ps]] - m_rows[ps])
            out.append(l_accs[ps] + _fold_sublanes(p, jnp.sum))
            v_blk = vt_ref[0, c, g * HEAD_DIM:(g + 1) * HEAD_DIM, :]
            o_ref[ps] += _dot(v_blk, p.astype(jnp.bfloat16))
        return tuple(out)

    l_accs = _tile_loop(
        0, nt, pv_body,
        tuple(jnp.zeros((SUBLANES, width), jnp.float32) for _ in range(n_pass)))
    for ps in range(n_pass):
        out = o_ref[ps] / jnp.sum(l_accs[ps], axis=0, keepdims=True)
        for r in range(hpp):
            h = ps * hpp + r
            at_ref[0, 0, h * HEAD_DIM:(h + 1) * HEAD_DIM, :] = (
                out[:, r * ATTN_QB:(r + 1) * ATTN_QB].astype(at_ref.dtype))


def _attn(qit, wit, qt, kin, kn, vt):
    b, nq = qt.shape[0], qt.shape[1]
    nt = kn.shape[1]
    per_q = lambda rows: pl.BlockSpec((1, 1, rows, ATTN_QB), lambda i, j: (i, j, 0, 0))
    per_b = lambda arr: pl.BlockSpec((1,) + arr.shape[1:], lambda i, j: (i, 0, 0, 0))
    tile = (nt, ATTN_KT, ATTN_QB)
    return pl.pallas_call(
        _attn_kernel,
        grid=(b, nq),
        in_specs=[per_q(IQ_COLS), per_q(SUBLANES), per_q(Q_COLS),
                  per_b(kin), per_b(kn), per_b(vt)],
        out_specs=per_q(ATTN_WIDTH),
        out_shape=jax.ShapeDtypeStruct((b, nq, ATTN_WIDTH, ATTN_QB), jnp.bfloat16),
        scratch_shapes=[
            pltpu.VMEM(tile, jnp.float32),
            pltpu.VMEM(tile, jnp.bfloat16),
            pltpu.VMEM(tile, jnp.float32),
            pltpu.VMEM((nt, ATTN_KT, N_HEADS * ATTN_QB), jnp.float32),
            pltpu.VMEM((SUBLANES, ATTN_QB), jnp.float32),
            pltpu.VMEM((SUBLANES, ATTN_QB), jnp.float32),
            pltpu.VMEM((N_KV_HEADS * HEAD_DIM, N_HEADS * ATTN_QB), jnp.bfloat16),
            pltpu.VMEM((N_HEADS // ATTN_HEADS_PER_PASS, HEAD_DIM,
                        ATTN_HEADS_PER_PASS * ATTN_QB), jnp.float32),
        ],
        compiler_params=pltpu.CompilerParams(
            dimension_semantics=("parallel", "arbitrary"),
            vmem_limit_bytes=VMEM_LIMIT),
        name="attn",
    )(qit, wit, qt, kin, kn, vt)


def _mixffn_kernel(alpha, x_ref, at_ref, bo_ref, p_ref, wo_ref, win_ref, wout_ref,
                   wple_ref, wgate_ref, g1_ref, b1_ref, g2_ref, b2_ref, o_ref, acc_ref):
    a_mix = jnp.concatenate([_dot_tn(at_ref[0, qb], wo_ref[0:ATTN_WIDTH, :])
                             for qb in range(at_ref.shape[1])], axis=0)
    mix = a_mix + _dot(bo_ref[0], wo_ref[ATTN_WIDTH:, :])
    x = _layer_norm(alpha * x_ref[0] + mix, g1_ref[...], b1_ref[...])
    xb = x.astype(jnp.bfloat16)
    gate_lin = _dot(xb, wgate_ref[...])
    ple = _dot(p_ref[0].astype(jnp.bfloat16), wple_ref[...]) * jax.nn.sigmoid(gate_lin)
    acc_ref[...] = alpha * x + ple
    for c in range(D_FF // FFN_CHUNK):
        cols = slice(c * FFN_CHUNK, (c + 1) * FFN_CHUNK)
        up_cols = slice(D_FF + c * FFN_CHUNK, D_FF + (c + 1) * FFN_CHUNK)
        gate = _dot(xb, win_ref[:, cols])
        up = _dot(xb, win_ref[:, up_cols])
        h = (jax.nn.silu(gate) * up).astype(jnp.bfloat16)
        acc_ref[...] += _dot(h, wout_ref[cols, :])
    o_ref[0] = _layer_norm(acc_ref[...], g2_ref[...], b2_ref[...])


def _mixffn(x, at, bo, p, wo, win, wout, wple, wgate, g1, b1, g2, b2, alpha):
    b, s, d = x.shape
    tm = FFN_TM
    const2 = lambda i, j: (0, 0)
    resident = lambda w: pl.BlockSpec(w.shape, const2, pipeline_mode=pl.Buffered(1))
    rows = lambda width: pl.BlockSpec((1, tm, width), lambda i, j: (i, j, 0))
    small = lambda v: pl.BlockSpec(v.shape, const2)
    return pl.pallas_call(
        functools.partial(_mixffn_kernel, alpha),
        grid=(b, s // tm),
        in_specs=[
            rows(d),
            pl.BlockSpec((1, tm // ATTN_QB, ATTN_WIDTH, ATTN_QB), lambda i, j: (i, j, 0, 0)),
            rows(SGU_WIDTH),
            rows(PLE_DIM),
            resident(wo), resident(win), resident(wout), resident(wple), resident(wgate),
            small(g1), small(b1), small(g2), small(b2),
        ],
        out_specs=rows(d),
        out_shape=jax.ShapeDtypeStruct(x.shape, x.dtype),
        scratch_shapes=[pltpu.VMEM((tm, d), jnp.float32)],
        compiler_params=pltpu.CompilerParams(
            dimension_semantics=("parallel", "parallel"),
            vmem_limit_bytes=VMEM_LIMIT),
        name="mixffn",
    )(x, at, bo, p, wo, win, wout, wple, wgate, g1, b1, g2, b2)


def kernel(x, p, positions, w_in, w_o, ln1_g, ln1_b, ln2_g, ln2_b, sgu_w, sgu_b,
           sgu_ln_g, sgu_ln_b, w_ffn_in, w_ffn_out, w_ple, w_ple_gate):
    depth = w_in.shape[0]
    alpha = (2 * depth) ** 0.25
    bf = jnp.bfloat16
    attn_cols = ROW_WI + N_IDX_HEADS

    cos_t, sin_t = _rope_tables(positions)

    for i in range(depth):
        wta = jnp.pad(w_in[i, :, :attn_cols].T.astype(bf),
                      ((0, ATTN_ROWS - attn_cols), (0, 0)))
        wsgu = w_in[i, :, attn_cols:].astype(bf)
        sgu_bias = jnp.repeat(sgu_b[i].T, SGU_GROUP_DIM, axis=1)

        qt, kn, vt, qit, kin, wit, bo = _proj(
            x, wta, wsgu, cos_t, sin_t, sgu_w[i], sgu_bias,
            sgu_ln_g[i][None], sgu_ln_b[i][None])
        at = _attn(qit, wit, qt, kin, kn, vt)
        x = _mixffn(x, at, bo, p[i], w_o[i].astype(bf),
                    w_ffn_in[i].astype(bf), w_ffn_out[i].astype(bf),
                    w_ple[i].astype(bf), w_ple_gate[i].astype(bf),
                    ln1_g[i][None], ln1_b[i][None], ln2_g[i][None], ln2_b[i][None], alpha)
    return x
```

```python
import functools
import math

import jax
import jax.numpy as jnp
from jax import lax
from jax.experimental import pallas as pl
from jax.experimental.pallas import tpu as pltpu

D_MODEL = 1024
CHUNK = 64
N_HEADS = 8
N_KV_HEADS = 2
HEAD_DIM = 64
ATTN_WIDTH = N_HEADS * HEAD_DIM
ROPE_DIM = HEAD_DIM // 4
ROPE_HALF = ROPE_DIM // 2
ROPE_THETA = 500000.0
N_IDX_HEADS = 4
IDX_DIM = 64
TOPK_MAX = 256
SGU_CHUNK = 128
SGU_GROUPS = 8
SGU_WIDTH = D_MODEL - ATTN_WIDTH
SGU_GROUP_DIM = SGU_WIDTH // SGU_GROUPS
D_FF = 2816
PLE_DIM = 256
LN_EPS = 1e-5

Q_COLS = N_HEADS * HEAD_DIM
KV_COLS = N_KV_HEADS * HEAD_DIM
IQ_COLS = N_IDX_HEADS * IDX_DIM

ROW_Q = 0
ROW_K = ROW_Q + Q_COLS
ROW_V = ROW_K + KV_COLS
ROW_QI = ROW_V + KV_COLS
ROW_KI = ROW_QI + IQ_COLS
ROW_WI = ROW_KI + IDX_DIM
ATTN_ROWS = 1152

LANES = 128
SUBLANES = 8
BF16_ROWS = 16
VMEM_LIMIT = 56 * 1024 * 1024

PROJ_TM = 512
FFN_TM = 512
FFN_CHUNK = 256
ATTN_QB = 256
ATTN_KT = 256
ATTN_HEADS_PER_PASS = 2
TILE_UNROLL = (4, 2, 1)
HI_BITS = 16
REFINE_BITS = 17
KEY_OF_NEG_INF = -0x7F800001
HI_KEY_OF_NEG_INF = -0x7F810000
MIN_NORMAL_KEY = 0x00800000
INT_MIN = -0x80000000
NEG_INF = float("-inf")
LOG2E = math.log2(math.e)


def _layer_norm(y, g, b):
    mu = jnp.mean(y, axis=-1, keepdims=True)
    d = y - mu
    var = jnp.mean(d * d, axis=-1, keepdims=True)
    return d * lax.rsqrt(var + LN_EPS) * g + b


def _dot(a, b):
    return jnp.dot(a, b, preferred_element_type=jnp.float32)


def _dot_nt(a, b):
    return lax.dot_general(a, b, (((1,), (1,)), ((), ())),
                           preferred_element_type=jnp.float32)


def _dot_tn(a, b):
    return lax.dot_general(a, b, (((0,), (0,)), ((), ())),
                           preferred_element_type=jnp.float32)


def _rope_table_kernel(pos_ref, cos_ref, sin_ref):
    seq = pos_ref.shape[-1]
    pos = pos_ref[0].astype(jnp.float32)
    i = lax.broadcasted_iota(jnp.int32, (ROPE_HALF, seq), 0).astype(jnp.float32)
    inv = jnp.power(jnp.float32(ROPE_THETA), i * (-2.0 / ROPE_DIM))
    ang = pos * inv
    cos_ref[0] = jnp.cos(ang)
    sin_ref[0] = jnp.sin(ang)


def _rope_tables(positions):
    b, s = positions.shape
    pos3 = positions.reshape(b, 1, s)
    out = jax.ShapeDtypeStruct((b, ROPE_HALF, s), jnp.float32)
    return pl.pallas_call(
        _rope_table_kernel,
        grid=(b,),
        in_specs=[pl.BlockSpec((1, 1, s), lambda i: (i, 0, 0))],
        out_specs=[pl.BlockSpec((1, ROPE_HALF, s), lambda i: (i, 0, 0))] * 2,
        out_shape=[out, out],
        name="rope_tables",
    )(pos3)


def _rope_rows(ht, cos, sin):
    x1 = ht[0:ROPE_HALF]
    x2 = ht[ROPE_HALF:ROPE_DIM]
    return jnp.concatenate(
        [x1 * cos - x2 * sin, x2 * cos + x1 * sin, ht[ROPE_DIM:]], axis=0)


def _proj_kernel(x_ref, wta_ref, wsgu_ref, cos_ref, sin_ref, sw_ref, sb_ref,
                 lng_ref, lnb_ref,
                 qt_ref, kn_ref, vt_ref, qit_ref, kin_ref, wit_ref, bo_ref):
    tm = x_ref.shape[1]
    xb = x_ref[0].astype(jnp.bfloat16)
    cos = cos_ref[0]
    sin = sin_ref[0]

    def store_query_blocks(ref, rows, val):
        for qb in range(tm // ATTN_QB):
            ref[0, qb, rows, :] = val[:, qb * ATTN_QB:(qb + 1) * ATTN_QB].astype(ref.dtype)

    q_scale = HEAD_DIM ** -0.5 * LOG2E
    qt = _dot_nt(wta_ref[ROW_Q:ROW_K, :], xb)
    for h in range(N_HEADS):
        head = _rope_rows(qt[h * HEAD_DIM:(h + 1) * HEAD_DIM], cos, sin)
        store_query_blocks(qt_ref, slice(h * HEAD_DIM, (h + 1) * HEAD_DIM), head * q_scale)

    kt = _dot_nt(wta_ref[ROW_K:ROW_V, :], xb)
    kt = jnp.concatenate(
        [_rope_rows(kt[g * HEAD_DIM:(g + 1) * HEAD_DIM], cos, sin)
         for g in range(N_KV_HEADS)], axis=0)
    kn = kt.T

    vt = _dot_nt(wta_ref[ROW_V:ROW_QI, :], xb)

    qit = _dot_nt(wta_ref[ROW_QI:ROW_KI, :], xb)
    for h in range(N_IDX_HEADS):
        head = _rope_rows(qit[h * IDX_DIM:(h + 1) * IDX_DIM], cos, sin)
        store_query_blocks(qit_ref, slice(h * IDX_DIM, (h + 1) * IDX_DIM), head)

    kwt = _dot_nt(wta_ref[ROW_KI:ATTN_ROWS, :], xb)
    kit = _rope_rows(kwt[0:IDX_DIM], cos, sin)
    kit = jnp.concatenate([kit, jnp.zeros_like(kit)], axis=0)
    kin = kit.T
    store_query_blocks(wit_ref, slice(0, SUBLANES),
                       kwt[IDX_DIM:IDX_DIM + SUBLANES] * ((N_IDX_HEADS * IDX_DIM) ** -0.5))

    for c in range(tm // ATTN_KT):
        rows = slice(c * ATTN_KT, (c + 1) * ATTN_KT)
        kn_ref[0, c] = kn[rows].astype(kn_ref.dtype)
        kin_ref[0, c] = kin[rows].astype(kin_ref.dtype)
        vt_ref[0, c] = vt[:, rows].astype(vt_ref.dtype)

    uv = _dot(xb, wsgu_ref[...])
    gu = jax.nn.gelu(uv[:, :SGU_WIDTH])
    gv = jax.nn.gelu(uv[:, SGU_WIDTH:])
    gv = _layer_norm(gv, lng_ref[...], lnb_ref[...]).astype(jnp.bfloat16)

    row = lax.broadcasted_iota(jnp.int32, (SGU_CHUNK, SGU_CHUNK), 0)
    col = lax.broadcasted_iota(jnp.int32, (SGU_CHUNK, SGU_CHUNK), 1)
    tri = col <= row
    ws = [jnp.where(tri, sw_ref[g], 0.0).astype(jnp.bfloat16)
          for g in range(SGU_GROUPS)]
    lane = lax.broadcasted_iota(jnp.int32, (SGU_CHUNK, LANES), 1)
    first_group = lane < SGU_GROUP_DIM
    for c in range(tm // SGU_CHUNK):
        rows = slice(c * SGU_CHUNK, (c + 1) * SGU_CHUNK)
        for pr in range(SGU_WIDTH // LANES):
            cols = slice(pr * LANES, (pr + 1) * LANES)
            vblk = gv[rows, cols]
            mixed = jnp.where(first_group,
                              _dot(ws[2 * pr], vblk), _dot(ws[2 * pr + 1], vblk))
            mixed = mixed + sb_ref[:, cols]
            bo_ref[0, rows, cols] = (gu[rows, cols] * mixed).astype(bo_ref.dtype)


def _layer_spec(stacked, layer, **kwargs):
    zeros = (0,) * (stacked.ndim - 1)
    return pl.BlockSpec((None,) + stacked.shape[1:], lambda *_: (layer,) + zeros, **kwargs)


def _proj(layer, x, wta, wsgu, cos_t, sin_t, sgu_w, sgu_bias, ln_g, ln_b):
    b, s, d = x.shape
    tm = PROJ_TM
    nt = s // ATTN_KT
    tpt = tm // ATTN_KT
    grid = (b, s // tm)
    in_specs = [
        pl.BlockSpec((1, tm, d), lambda i, j: (i, j, 0)),
        _layer_spec(wta, layer),
        _layer_spec(wsgu, layer),
        pl.BlockSpec((1, ROPE_HALF, tm), lambda i, j: (i, 0, j)),
        pl.BlockSpec((1, ROPE_HALF, tm), lambda i, j: (i, 0, j)),
        _layer_spec(sgu_w, layer),
        _layer_spec(sgu_bias, layer),
        _layer_spec(ln_g, layer),
        _layer_spec(ln_b, layer),
    ]
    key_rows = pl.BlockSpec((1, tpt, ATTN_KT, LANES), lambda i, j: (i, j, 0, 0))
    key_cols = pl.BlockSpec((1, tpt, LANES, ATTN_KT), lambda i, j: (i, j, 0, 0))
    query_blocks = lambda rows: pl.BlockSpec((1, tm // ATTN_QB, rows, ATTN_QB),
                                             lambda i, j: (i, j, 0, 0))
    out_specs = [
        query_blocks(Q_COLS),
        key_rows,
        key_cols,
        query_blocks(IQ_COLS),
        key_rows,
        query_blocks(SUBLANES),
        pl.BlockSpec((1, tm, SGU_WIDTH), lambda i, j: (i, j, 0)),
    ]
    bf = jnp.bfloat16
    nq = s // ATTN_QB
    out_shape = [
        jax.ShapeDtypeStruct((b, nq, Q_COLS, ATTN_QB), bf),
        jax.ShapeDtypeStruct((b, nt, ATTN_KT, LANES), bf),
        jax.ShapeDtypeStruct((b, nt, LANES, ATTN_KT), bf),
        jax.ShapeDtypeStruct((b, nq, IQ_COLS, ATTN_QB), bf),
        jax.ShapeDtypeStruct((b, nt, ATTN_KT, LANES), bf),
        jax.ShapeDtypeStruct((b, nq, SUBLANES, ATTN_QB), jnp.float32),
        jax.ShapeDtypeStruct((b, s, SGU_WIDTH), bf),
    ]
    return pl.pallas_call(
        _proj_kernel,
        grid=grid,
        in_specs=in_specs,
        out_specs=out_specs,
        out_shape=out_shape,
        compiler_params=pltpu.CompilerParams(
            dimension_semantics=("parallel", "parallel"),
            vmem_limit_bytes=VMEM_LIMIT),
        name="proj",
    )(x, wta, wsgu, cos_t, sin_t, sgu_w, sgu_bias, ln_g, ln_b)


def _fold_sublanes(x, op):
    return op(x.reshape(x.shape[0] // SUBLANES, SUBLANES, x.shape[1]), axis=0)


def _tile_loop(lo, hi, body, init):
    carry, start = init, lo
    for width in TILE_UNROLL:
        def group(i, c, width=width, start=start):
            for u in range(width):
                c = body(start + i * width + u, c)
            return c
        n_groups = (hi - start) // width
        carry = lax.fori_loop(0, n_groups, group, carry)
        start = start + n_groups * width
    return carry


def _count_bf16(ref, n_tiles, pred):
    one = jnp.ones((), jnp.bfloat16)
    zero = jnp.zeros((), jnp.bfloat16)
    groups = ATTN_KT // BF16_ROWS
    n_acc = 4
    assert groups * n_tiles <= 256 * n_acc
    accs = [None] * n_acc
    for c in range(n_tiles):
        m3 = jnp.where(pred(ref[c]), one, zero).reshape(groups, BF16_ROWS, ATTN_QB)
        for i in range(groups):
            k = i % n_acc
            accs[k] = m3[i] if accs[k] is None else accs[k] + m3[i]
    acc = (accs[0] + accs[1]) + (accs[2] + accs[3])
    return jnp.sum(acc.astype(jnp.float32), axis=0, keepdims=True)


def _bit_search(n_bits, count_ge, target):
    def body(i, val):
        trial = val | jnp.left_shift(jnp.int32(1), (n_bits - 1) - i)
        return jnp.where(count_ge(trial) >= target, trial, val)
    return lax.fori_loop(0, n_bits, body, jnp.zeros((1, ATTN_QB), jnp.int32))


def _key_to_float(key):
    bits = key ^ ((key >> 31) & jnp.int32(0x7FFFFFFF))
    return pltpu.bitcast(bits, jnp.float32)


def _hi_bits(uhi):
    key = jnp.maximum(jnp.left_shift(uhi, HI_BITS) ^ jnp.int32(INT_MIN), HI_KEY_OF_NEG_INF)
    key = jnp.where(key > 0, jnp.maximum(key, MIN_NORMAL_KEY), key)
    return key ^ ((key >> 31) & jnp.int32(0x7FFF0000))


def _hi_to_bf16(uhi):
    return pltpu.bitcast(_hi_bits(uhi), jnp.float32).astype(jnp.bfloat16)


def _count_f32(ref, n_tiles, pred):
    parts = [_fold_sublanes(jnp.where(pred(ref[c]), 1.0, 0.0), jnp.sum)
             for c in range(n_tiles)]
    return jnp.sum(functools.reduce(lambda a, b: a + b, parts), axis=0, keepdims=True)


def _threshold_search(n_tiles, score_ref, hb_ref, thr_ref, room_ref):
    target = jnp.float32(TOPK_MAX)

    def coarse_count(uhi):
        cand = _hi_to_bf16(uhi)
        return _count_bf16(hb_ref, n_tiles, lambda blk: blk >= cand)

    uhi = _bit_search(HI_BITS, coarse_count, target)
    bits = _hi_bits(uhi)
    base = (bits ^ ((bits >> 31) & jnp.int32(0x7FFFFFFF))) - (1 << (HI_BITS - 1))

    def fine_count(offset):
        cand = _key_to_float(jnp.maximum(base + offset, KEY_OF_NEG_INF))
        return _count_f32(score_ref, n_tiles, lambda blk: blk >= cand)

    offset = _bit_search(REFINE_BITS, fine_count, target)
    thr = _key_to_float(jnp.maximum(base + offset, KEY_OF_NEG_INF + 1))
    room = target - _count_f32(score_ref, n_tiles, lambda blk: blk > thr)
    thr_ref[...] = jnp.broadcast_to(thr, thr_ref.shape)
    room_ref[...] = jnp.broadcast_to(room, room_ref.shape)


def _attn_kernel(qit_ref, wit_ref, qt_ref, kin_ref, kn_ref, vt_ref, at_ref,
                 score_ref, hb_ref, bias_ref, s_ref, thr_ref, room_ref,
                 qpad_ref, o_ref):
    j = pl.program_id(1)
    nt = j + 1
    max_tiles = score_ref.shape[0]

    zeros_i = jnp.zeros((LANES - IDX_DIM, ATTN_QB), jnp.bfloat16)
    qi_pad = [jnp.concatenate([qit_ref[0, 0, h * IDX_DIM:(h + 1) * IDX_DIM, :], zeros_i], axis=0)
              for h in range(N_IDX_HEADS)]
    wit = wit_ref[0, 0]

    def tile_scores(c):
        kt = kin_ref[0, c]
        sc = jnp.zeros((ATTN_KT, ATTN_QB), jnp.float32)
        for h in range(N_IDX_HEADS):
            sc = sc + jnp.maximum(_dot(kt, qi_pad[h]), 0.0) * wit[h:h + 1, :]
        return sc

    def store_scores(c, sc):
        score_ref[c] = sc
        hb_ref[c] = sc.astype(jnp.bfloat16)

    def score_body(c, carry):
        store_scores(c, tile_scores(c))
        return carry

    _tile_loop(0, j, score_body, 0)
    kchunk = lax.broadcasted_iota(jnp.int32, (ATTN_KT, ATTN_QB), 0) // CHUNK
    qchunk = lax.broadcasted_iota(jnp.int32, (ATTN_KT, ATTN_QB), 1) // CHUNK
    store_scores(j, jnp.where(kchunk > qchunk, NEG_INF, tile_scores(j)))

    @pl.when(nt * ATTN_KT <= TOPK_MAX)
    def _():
        def body(c, carry):
            bias_ref[c] = jnp.where(score_ref[c] == NEG_INF, NEG_INF, 0.0)
            return carry
        lax.fori_loop(0, nt, body, 0)

    for n_tiles in range(TOPK_MAX // ATTN_KT + 1, max_tiles + 1):
        pl.when(nt == n_tiles)(functools.partial(
            _threshold_search, n_tiles, score_ref, hb_ref, thr_ref, room_ref))

    @pl.when(nt * ATTN_KT > TOPK_MAX)
    def _():
        thr = thr_ref[0:1, :]
        room = room_ref[0:1, :]
        krow = lax.broadcasted_iota(jnp.int32, (ATTN_KT, ATTN_KT), 0)
        kcol = lax.broadcasted_iota(jnp.int32, (ATTN_KT, ATTN_KT), 1)
        incl = (kcol <= krow).astype(jnp.bfloat16)

        def tie_body(c, seen):
            sc = score_ref[c]
            tie = sc == thr
            rank = _dot(incl, jnp.where(tie, 1.0, 0.0).astype(jnp.bfloat16)) + seen
            keep_tie = jnp.where(tie, jnp.where(rank <= room, 0.0, NEG_INF), NEG_INF)
            bias_ref[c] = jnp.where(sc > thr, 0.0, keep_tie)
            return rank[ATTN_KT - 1:ATTN_KT, :]

        _tile_loop(0, nt, tie_body, jnp.zeros((1, ATTN_QB), jnp.float32))

    rep = N_HEADS // N_KV_HEADS
    hpp = ATTN_HEADS_PER_PASS
    width = hpp * ATTN_QB
    n_pass = N_HEADS // hpp
    pass_cols = [slice(ps * width, (ps + 1) * width) for ps in range(n_pass)]
    pass_group = [(ps * hpp) // rep for ps in range(n_pass)]

    qpad_ref[...] = jnp.zeros(qpad_ref.shape, qpad_ref.dtype)
    for h in range(N_HEADS):
        g = h // rep
        qpad_ref[g * HEAD_DIM:(g + 1) * HEAD_DIM, h * ATTN_QB:(h + 1) * ATTN_QB] = (
            qt_ref[0, 0, h * HEAD_DIM:(h + 1) * HEAD_DIM, :])

    def qk_body(c, m_accs):
        kt = kn_ref[0, c]
        bias = jnp.concatenate([bias_ref[c]] * hpp, axis=1)
        out = []
        for ps in range(n_pass):
            s = _dot(kt, qpad_ref[:, pass_cols[ps]]) + bias
            s_ref[c, :, pass_cols[ps]] = s
            out.append(jnp.maximum(m_accs[ps], _fold_sublanes(s, jnp.max)))
        return tuple(out)

    m_accs = _tile_loop(
        0, nt, qk_body,
        tuple(jnp.full((SUBLANES, width), NEG_INF, jnp.float32) for _ in range(n_pass)))
    m_rows = [jnp.max(m, axis=0, keepdims=True) for m in m_accs]

    o_ref[...] = jnp.zeros(o_ref.shape, o_ref.dtype)

    def pv_body(c, l_accs):
        out = []
        for ps in range(n_pass):
            g = pass_group[ps]
            p = jnp.exp2(s_ref[c, :, pass_cols[ps]] - m_rows[ps])
            out.append(l_accs[ps] + _fold_sublanes(p, jnp.sum))
            v_blk = vt_ref[0, c, g * HEAD_DIM:(g + 1) * HEAD_DIM, :]
            o_ref[ps] += _dot(v_blk, p.astype(jnp.bfloat16))
        return tuple(out)

    l_accs = _tile_loop(
        0, nt, pv_body,
        tuple(jnp.zeros((SUBLANES, width), jnp.float32) for _ in range(n_pass)))
    for ps in range(n_pass):
        out = o_ref[ps] / jnp.sum(l_accs[ps], axis=0, keepdims=True)
        for r in range(hpp):
            h = ps * hpp + r
            at_ref[0, 0, h * HEAD_DIM:(h + 1) * HEAD_DIM, :] = (
                out[:, r * ATTN_QB:(r + 1) * ATTN_QB].astype(at_ref.dtype))


def _attn(qit, wit, qt, kin, kn, vt):
    b, nq = qt.shape[0], qt.shape[1]
    nt = kn.shape[1]
    per_q = lambda rows: pl.BlockSpec((1, 1, rows, ATTN_QB), lambda i, j: (i, j, 0, 0))
    per_b = lambda arr: pl.BlockSpec((1,) + arr.shape[1:], lambda i, j: (i, 0, 0, 0))
    tile = (nt, ATTN_KT, ATTN_QB)
    return pl.pallas_call(
        _attn_kernel,
        grid=(b, nq),
        in_specs=[per_q(IQ_COLS), per_q(SUBLANES), per_q(Q_COLS),
                  per_b(kin), per_b(kn), per_b(vt)],
        out_specs=per_q(ATTN_WIDTH),
        out_shape=jax.ShapeDtypeStruct((b, nq, ATTN_WIDTH, ATTN_QB), jnp.bfloat16),
        scratch_shapes=[
            pltpu.VMEM(tile, jnp.float32),
            pltpu.VMEM(tile, jnp.bfloat16),
            pltpu.VMEM(tile, jnp.float32),
            pltpu.VMEM((nt, ATTN_KT, N_HEADS * ATTN_QB), jnp.float32),
            pltpu.VMEM((SUBLANES, ATTN_QB), jnp.float32),
            pltpu.VMEM((SUBLANES, ATTN_QB), jnp.float32),
            pltpu.VMEM((N_KV_HEADS * HEAD_DIM, N_HEADS * ATTN_QB), jnp.bfloat16),
            pltpu.VMEM((N_HEADS // ATTN_HEADS_PER_PASS, HEAD_DIM,
                        ATTN_HEADS_PER_PASS * ATTN_QB), jnp.float32),
        ],
        compiler_params=pltpu.CompilerParams(
            dimension_semantics=("parallel", "arbitrary"),
            vmem_limit_bytes=VMEM_LIMIT),
        name="attn",
    )(qit, wit, qt, kin, kn, vt)


def _mixffn_kernel(alpha, x_ref, at_ref, bo_ref, p_ref, wo_ref, win_ref, wout_ref,
                   wple_ref, wgate_ref, g1_ref, b1_ref, g2_ref, b2_ref, o_ref, acc_ref):
    a_mix = jnp.concatenate([_dot_tn(at_ref[0, qb], wo_ref[0:ATTN_WIDTH, :])
                             for qb in range(at_ref.shape[1])], axis=0)
    mix = a_mix + _dot(bo_ref[0], wo_ref[ATTN_WIDTH:, :])
    x = _layer_norm(alpha * x_ref[0] + mix, g1_ref[...], b1_ref[...])
    xb = x.astype(jnp.bfloat16)
    gate_lin = _dot(xb, wgate_ref[...])
    ple = _dot(p_ref[0].astype(jnp.bfloat16), wple_ref[...]) * jax.nn.sigmoid(gate_lin)
    acc_ref[...] = alpha * x + ple
    for c in range(D_FF // FFN_CHUNK):
        cols = slice(c * FFN_CHUNK, (c + 1) * FFN_CHUNK)
        up_cols = slice(D_FF + c * FFN_CHUNK, D_FF + (c + 1) * FFN_CHUNK)
        gate = _dot(xb, win_ref[:, cols])
        up = _dot(xb, win_ref[:, up_cols])
        h = (jax.nn.silu(gate) * up).astype(jnp.bfloat16)
        acc_ref[...] += _dot(h, wout_ref[cols, :])
    o_ref[0] = _layer_norm(acc_ref[...], g2_ref[...], b2_ref[...])


def _mixffn(layer, x, at, bo, p, wo, win, wout, wple, wgate, g1, b1, g2, b2, alpha):
    b, s, d = x.shape
    tm = FFN_TM
    resident = lambda w: _layer_spec(w, layer, pipeline_mode=pl.Buffered(1))
    rows = lambda width: pl.BlockSpec((1, tm, width), lambda i, j: (i, j, 0))
    return pl.pallas_call(
        functools.partial(_mixffn_kernel, alpha),
        grid=(b, s // tm),
        in_specs=[
            rows(d),
            pl.BlockSpec((1, tm // ATTN_QB, ATTN_WIDTH, ATTN_QB), lambda i, j: (i, j, 0, 0)),
            rows(SGU_WIDTH),
            pl.BlockSpec((None, 1, tm, PLE_DIM), lambda i, j: (layer, i, j, 0)),
            resident(wo), resident(win), resident(wout), resident(wple), resident(wgate),
            _layer_spec(g1, layer), _layer_spec(b1, layer),
            _layer_spec(g2, layer), _layer_spec(b2, layer),
        ],
        out_specs=rows(d),
        out_shape=jax.ShapeDtypeStruct(x.shape, x.dtype),
        scratch_shapes=[pltpu.VMEM((tm, d), jnp.float32)],
        compiler_params=pltpu.CompilerParams(
            dimension_semantics=("parallel", "parallel"),
            vmem_limit_bytes=VMEM_LIMIT),
        name="mixffn",
    )(x, at, bo, p, wo, win, wout, wple, wgate, g1, b1, g2, b2)


def kernel(x, p, positions, w_in, w_o, ln1_g, ln1_b, ln2_g, ln2_b, sgu_w, sgu_b,
           sgu_ln_g, sgu_ln_b, w_ffn_in, w_ffn_out, w_ple, w_ple_gate):
    depth = w_in.shape[0]
    alpha = (2 * depth) ** 0.25
    bf = jnp.bfloat16
    attn_cols = ROW_WI + N_IDX_HEADS

    cos_t, sin_t = _rope_tables(positions)

    w_in_bf = w_in.astype(bf)
    wta = jnp.pad(jnp.swapaxes(w_in_bf[:, :, :attn_cols], 1, 2),
                  ((0, 0), (0, ATTN_ROWS - attn_cols), (0, 0)))
    wsgu = w_in_bf[:, :, attn_cols:]
    sgu_bias = jnp.repeat(jnp.swapaxes(sgu_b, 1, 2), SGU_GROUP_DIM, axis=2)
    row = lambda v: v[:, None, :]
    wo, win, wout = w_o.astype(bf), w_ffn_in.astype(bf), w_ffn_out.astype(bf)
    wple, wgate = w_ple.astype(bf), w_ple_gate.astype(bf)

    for i in range(depth):
        qt, kn, vt, qit, kin, wit, bo = _proj(
            i, x, wta, wsgu, cos_t, sin_t, sgu_w, sgu_bias, row(sgu_ln_g), row(sgu_ln_b))
        at = _attn(qit, wit, qt, kin, kn, vt)
        x = _mixffn(i, x, at, bo, p, wo, win, wout, wple, wgate,
                    row(ln1_g), row(ln1_b), row(ln2_g), row(ln2_b), alpha)
    return x
```

```python
import functools
import math

import jax
import jax.numpy as jnp
from jax import lax
from jax.experimental import pallas as pl
from jax.experimental.pallas import tpu as pltpu

D_MODEL = 1024
CHUNK = 64
N_HEADS = 8
N_KV_HEADS = 2
HEAD_DIM = 64
ATTN_WIDTH = N_HEADS * HEAD_DIM
ROPE_DIM = HEAD_DIM // 4
ROPE_HALF = ROPE_DIM // 2
ROPE_THETA = 500000.0
N_IDX_HEADS = 4
IDX_DIM = 64
TOPK_MAX = 256
SGU_CHUNK = 128
SGU_GROUPS = 8
SGU_WIDTH = D_MODEL - ATTN_WIDTH
SGU_GROUP_DIM = SGU_WIDTH // SGU_GROUPS
D_FF = 2816
PLE_DIM = 256
LN_EPS = 1e-5

Q_COLS = N_HEADS * HEAD_DIM
KV_COLS = N_KV_HEADS * HEAD_DIM
IQ_COLS = N_IDX_HEADS * IDX_DIM

ROW_Q = 0
ROW_K = ROW_Q + Q_COLS
ROW_V = ROW_K + KV_COLS
ROW_QI = ROW_V + KV_COLS
ROW_KI = ROW_QI + IQ_COLS
ROW_WI = ROW_KI + IDX_DIM
ATTN_ROWS = 1152

LANES = 128
SUBLANES = 8
BF16_ROWS = 16
VMEM_LIMIT = 56 * 1024 * 1024

PROJ_TM = 1024
FFN_TM = 1024
FFN_CHUNK = 256
ATTN_QB = 256
ATTN_KT = 256
ATTN_HEADS_PER_PASS = 2
TILE_UNROLL = (4, 2, 1)
HI_BITS = 16
REFINE_BITS = 17
KEY_OF_NEG_INF = -0x7F800001
HI_KEY_OF_NEG_INF = -0x7F810000
MIN_NORMAL_KEY = 0x00800000
INT_MIN = -0x80000000
NEG_INF = float("-inf")
LOG2E = math.log2(math.e)


def _layer_norm(y, g, b):
    mu = jnp.mean(y, axis=-1, keepdims=True)
    d = y - mu
    var = jnp.mean(d * d, axis=-1, keepdims=True)
    return d * lax.rsqrt(var + LN_EPS) * g + b


def _dot(a, b):
    return jnp.dot(a, b, preferred_element_type=jnp.float32)


def _dot_nt(a, b):
    return lax.dot_general(a, b, (((1,), (1,)), ((), ())),
                           preferred_element_type=jnp.float32)


def _dot_tn(a, b):
    return lax.dot_general(a, b, (((0,), (0,)), ((), ())),
                           preferred_element_type=jnp.float32)


def _rope_table_kernel(pos_ref, cos_ref, sin_ref):
    seq = pos_ref.shape[-1]
    pos = pos_ref[0].astype(jnp.float32)
    i = lax.broadcasted_iota(jnp.int32, (ROPE_HALF, seq), 0).astype(jnp.float32)
    inv = jnp.power(jnp.float32(ROPE_THETA), i * (-2.0 / ROPE_DIM))
    ang = pos * inv
    cos_ref[0] = jnp.cos(ang)
    sin_ref[0] = jnp.sin(ang)


def _rope_tables(positions):
    b, s = positions.shape
    pos3 = positions.reshape(b, 1, s)
    out = jax.ShapeDtypeStruct((b, ROPE_HALF, s), jnp.float32)
    return pl.pallas_call(
        _rope_table_kernel,
        grid=(b,),
        in_specs=[pl.BlockSpec((1, 1, s), lambda i: (i, 0, 0))],
        out_specs=[pl.BlockSpec((1, ROPE_HALF, s), lambda i: (i, 0, 0))] * 2,
        out_shape=[out, out],
        name="rope_tables",
    )(pos3)


def _rope_rows(ht, cos, sin):
    x1 = ht[0:ROPE_HALF]
    x2 = ht[ROPE_HALF:ROPE_DIM]
    return jnp.concatenate(
        [x1 * cos - x2 * sin, x2 * cos + x1 * sin, ht[ROPE_DIM:]], axis=0)


def _proj_kernel(x_ref, wta_ref, wsgu_ref, cos_ref, sin_ref, sw_ref, sb_ref,
                 lng_ref, lnb_ref,
                 qt_ref, kn_ref, vt_ref, qit_ref, kin_ref, wit_ref, bo_ref):
    tm = x_ref.shape[1]
    xb = x_ref[0].astype(jnp.bfloat16)
    cos = cos_ref[0]
    sin = sin_ref[0]

    def store_query_blocks(ref, rows, val):
        for qb in range(tm // ATTN_QB):
            ref[0, qb, rows, :] = val[:, qb * ATTN_QB:(qb + 1) * ATTN_QB].astype(ref.dtype)

    q_scale = HEAD_DIM ** -0.5 * LOG2E
    qt = _dot_nt(wta_ref[ROW_Q:ROW_K, :], xb)
    for h in range(N_HEADS):
        head = _rope_rows(qt[h * HEAD_DIM:(h + 1) * HEAD_DIM], cos, sin)
        store_query_blocks(qt_ref, slice(h * HEAD_DIM, (h + 1) * HEAD_DIM), head * q_scale)

    kt = _dot_nt(wta_ref[ROW_K:ROW_V, :], xb)
    kt = jnp.concatenate(
        [_rope_rows(kt[g * HEAD_DIM:(g + 1) * HEAD_DIM], cos, sin)
         for g in range(N_KV_HEADS)], axis=0)
    kn = kt.T

    vt = _dot_nt(wta_ref[ROW_V:ROW_QI, :], xb)

    qit = _dot_nt(wta_ref[ROW_QI:ROW_KI, :], xb)
    for h in range(N_IDX_HEADS):
        head = _rope_rows(qit[h * IDX_DIM:(h + 1) * IDX_DIM], cos, sin)
        store_query_blocks(qit_ref, slice(h * IDX_DIM, (h + 1) * IDX_DIM), head)

    kwt = _dot_nt(wta_ref[ROW_KI:ATTN_ROWS, :], xb)
    kit = _rope_rows(kwt[0:IDX_DIM], cos, sin)
    kit = jnp.concatenate([kit, jnp.zeros_like(kit)], axis=0)
    kin = kit.T
    store_query_blocks(wit_ref, slice(0, SUBLANES),
                       kwt[IDX_DIM:IDX_DIM + SUBLANES] * ((N_IDX_HEADS * IDX_DIM) ** -0.5))

    for c in range(tm // ATTN_KT):
        rows = slice(c * ATTN_KT, (c + 1) * ATTN_KT)
        kn_ref[0, c] = kn[rows].astype(kn_ref.dtype)
        kin_ref[0, c] = kin[rows].astype(kin_ref.dtype)
        vt_ref[0, c] = vt[:, rows].astype(vt_ref.dtype)

    uv = _dot(xb, wsgu_ref[...])
    gu = jax.nn.gelu(uv[:, :SGU_WIDTH])
    gv = jax.nn.gelu(uv[:, SGU_WIDTH:])
    gv = _layer_norm(gv, lng_ref[...], lnb_ref[...]).astype(jnp.bfloat16)

    row = lax.broadcasted_iota(jnp.int32, (SGU_CHUNK, SGU_CHUNK), 0)
    col = lax.broadcasted_iota(jnp.int32, (SGU_CHUNK, SGU_CHUNK), 1)
    tri = col <= row
    ws = [jnp.where(tri, sw_ref[g], 0.0).astype(jnp.bfloat16)
          for g in range(SGU_GROUPS)]
    lane = lax.broadcasted_iota(jnp.int32, (SGU_CHUNK, LANES), 1)
    first_group = lane < SGU_GROUP_DIM
    for c in range(tm // SGU_CHUNK):
        rows = slice(c * SGU_CHUNK, (c + 1) * SGU_CHUNK)
        for pr in range(SGU_WIDTH // LANES):
            cols = slice(pr * LANES, (pr + 1) * LANES)
            vblk = gv[rows, cols]
            mixed = jnp.where(first_group,
                              _dot(ws[2 * pr], vblk), _dot(ws[2 * pr + 1], vblk))
            mixed = mixed + sb_ref[:, cols]
            bo_ref[0, rows, cols] = (gu[rows, cols] * mixed).astype(bo_ref.dtype)


def _layer_spec(stacked, layer, **kwargs):
    zeros = (0,) * (stacked.ndim - 1)
    return pl.BlockSpec((None,) + stacked.shape[1:], lambda *_: (layer,) + zeros, **kwargs)


def _proj(layer, x, wta, wsgu, cos_t, sin_t, sgu_w, sgu_bias, ln_g, ln_b):
    b, s, d = x.shape
    tm = PROJ_TM
    nt = s // ATTN_KT
    tpt = tm // ATTN_KT
    grid = (b, s // tm)
    in_specs = [
        pl.BlockSpec((1, tm, d), lambda i, j: (i, j, 0)),
        _layer_spec(wta, layer),
        _layer_spec(wsgu, layer),
        pl.BlockSpec((1, ROPE_HALF, tm), lambda i, j: (i, 0, j)),
        pl.BlockSpec((1, ROPE_HALF, tm), lambda i, j: (i, 0, j)),
        _layer_spec(sgu_w, layer),
        _layer_spec(sgu_bias, layer),
        _layer_spec(ln_g, layer),
        _layer_spec(ln_b, layer),
    ]
    key_rows = pl.BlockSpec((1, tpt, ATTN_KT, LANES), lambda i, j: (i, j, 0, 0))
    key_cols = pl.BlockSpec((1, tpt, LANES, ATTN_KT), lambda i, j: (i, j, 0, 0))
    query_blocks = lambda rows: pl.BlockSpec((1, tm // ATTN_QB, rows, ATTN_QB),
                                             lambda i, j: (i, j, 0, 0))
    out_specs = [
        query_blocks(Q_COLS),
        key_rows,
        key_cols,
        query_blocks(IQ_COLS),
        key_rows,
        query_blocks(SUBLANES),
        pl.BlockSpec((1, tm, SGU_WIDTH), lambda i, j: (i, j, 0)),
    ]
    bf = jnp.bfloat16
    nq = s // ATTN_QB
    out_shape = [
        jax.ShapeDtypeStruct((b, nq, Q_COLS, ATTN_QB), bf),
        jax.ShapeDtypeStruct((b, nt, ATTN_KT, LANES), bf),
        jax.ShapeDtypeStruct((b, nt, LANES, ATTN_KT), bf),
        jax.ShapeDtypeStruct((b, nq, IQ_COLS, ATTN_QB), bf),
        jax.ShapeDtypeStruct((b, nt, ATTN_KT, LANES), bf),
        jax.ShapeDtypeStruct((b, nq, SUBLANES, ATTN_QB), jnp.float32),
        jax.ShapeDtypeStruct((b, s, SGU_WIDTH), bf),
    ]
    return pl.pallas_call(
        _proj_kernel,
        grid=grid,
        in_specs=in_specs,
        out_specs=out_specs,
        out_shape=out_shape,
        compiler_params=pltpu.CompilerParams(
            dimension_semantics=("parallel", "parallel"),
            vmem_limit_bytes=VMEM_LIMIT),
        name="proj",
    )(x, wta, wsgu, cos_t, sin_t, sgu_w, sgu_bias, ln_g, ln_b)


def _fold_sublanes(x, op):
    return op(x.reshape(x.shape[0] // SUBLANES, SUBLANES, x.shape[1]), axis=0)


def _tile_loop(lo, hi, body, init):
    carry, start = init, lo
    for width in TILE_UNROLL:
        def group(i, c, width=width, start=start):
            for u in range(width):
                c = body(start + i * width + u, c)
            return c
        n_groups = (hi - start) // width
        carry = lax.fori_loop(0, n_groups, group, carry)
        start = start + n_groups * width
    return carry


def _count_bf16(ref, n_tiles, pred):
    one = jnp.ones((), jnp.bfloat16)
    zero = jnp.zeros((), jnp.bfloat16)
    groups = ATTN_KT // BF16_ROWS
    n_acc = 4
    assert groups * n_tiles <= 256 * n_acc
    accs = [None] * n_acc
    for c in range(n_tiles):
        m3 = jnp.where(pred(ref[c]), one, zero).reshape(groups, BF16_ROWS, ATTN_QB)
        for i in range(groups):
            k = i % n_acc
            accs[k] = m3[i] if accs[k] is None else accs[k] + m3[i]
    acc = (accs[0] + accs[1]) + (accs[2] + accs[3])
    return jnp.sum(acc.astype(jnp.float32), axis=0, keepdims=True)


def _bit_search(n_bits, count_ge, target):
    def body(i, val):
        trial = val | jnp.left_shift(jnp.int32(1), (n_bits - 1) - i)
        return jnp.where(count_ge(trial) >= target, trial, val)
    return lax.fori_loop(0, n_bits, body, jnp.zeros((1, ATTN_QB), jnp.int32))


def _key_to_float(key):
    bits = key ^ ((key >> 31) & jnp.int32(0x7FFFFFFF))
    return pltpu.bitcast(bits, jnp.float32)


def _hi_bits(uhi):
    key = jnp.maximum(jnp.left_shift(uhi, HI_BITS) ^ jnp.int32(INT_MIN), HI_KEY_OF_NEG_INF)
    key = jnp.where(key > 0, jnp.maximum(key, MIN_NORMAL_KEY), key)
    return key ^ ((key >> 31) & jnp.int32(0x7FFF0000))


def _hi_to_bf16(uhi):
    return pltpu.bitcast(_hi_bits(uhi), jnp.float32).astype(jnp.bfloat16)


def _count_f32(ref, n_tiles, pred):
    parts = [_fold_sublanes(jnp.where(pred(ref[c]), 1.0, 0.0), jnp.sum)
             for c in range(n_tiles)]
    return jnp.sum(functools.reduce(lambda a, b: a + b, parts), axis=0, keepdims=True)


def _threshold_search(n_tiles, score_ref, hb_ref, thr_ref, room_ref):
    target = jnp.float32(TOPK_MAX)

    def coarse_count(uhi):
        cand = _hi_to_bf16(uhi)
        return _count_bf16(hb_ref, n_tiles, lambda blk: blk >= cand)

    uhi = _bit_search(HI_BITS, coarse_count, target)
    bits = _hi_bits(uhi)
    base = (bits ^ ((bits >> 31) & jnp.int32(0x7FFFFFFF))) - (1 << (HI_BITS - 1))

    def fine_count(offset):
        cand = _key_to_float(jnp.maximum(base + offset, KEY_OF_NEG_INF))
        return _count_f32(score_ref, n_tiles, lambda blk: blk >= cand)

    offset = _bit_search(REFINE_BITS, fine_count, target)
    thr = _key_to_float(jnp.maximum(base + offset, KEY_OF_NEG_INF + 1))
    room = target - _count_f32(score_ref, n_tiles, lambda blk: blk > thr)
    thr_ref[...] = jnp.broadcast_to(thr, thr_ref.shape)
    room_ref[...] = jnp.broadcast_to(room, room_ref.shape)


def _attn_kernel(qit_ref, wit_ref, qt_ref, kin_ref, kn_ref, vt_ref, at_ref,
                 score_ref, hb_ref, bias_ref, s_ref, thr_ref, room_ref,
                 qpad_ref, o_ref):
    j = pl.program_id(1)
    nt = j + 1
    max_tiles = score_ref.shape[0]

    zeros_i = jnp.zeros((LANES - IDX_DIM, ATTN_QB), jnp.bfloat16)
    qi_pad = [jnp.concatenate([qit_ref[0, 0, h * IDX_DIM:(h + 1) * IDX_DIM, :], zeros_i], axis=0)
              for h in range(N_IDX_HEADS)]
    wit = wit_ref[0, 0]

    def tile_scores(c):
        kt = kin_ref[0, c]
        sc = jnp.zeros((ATTN_KT, ATTN_QB), jnp.float32)
        for h in range(N_IDX_HEADS):
            sc = sc + jnp.maximum(_dot(kt, qi_pad[h]), 0.0) * wit[h:h + 1, :]
        return sc

    def store_scores(c, sc):
        score_ref[c] = sc
        hb_ref[c] = sc.astype(jnp.bfloat16)

    def score_body(c, carry):
        store_scores(c, tile_scores(c))
        return carry

    _tile_loop(0, j, score_body, 0)
    kchunk = lax.broadcasted_iota(jnp.int32, (ATTN_KT, ATTN_QB), 0) // CHUNK
    qchunk = lax.broadcasted_iota(jnp.int32, (ATTN_KT, ATTN_QB), 1) // CHUNK
    store_scores(j, jnp.where(kchunk > qchunk, NEG_INF, tile_scores(j)))

    @pl.when(nt * ATTN_KT <= TOPK_MAX)
    def _():
        def body(c, carry):
            bias_ref[c] = jnp.where(score_ref[c] == NEG_INF, NEG_INF, 0.0)
            return carry
        lax.fori_loop(0, nt, body, 0)

    for n_tiles in range(TOPK_MAX // ATTN_KT + 1, max_tiles + 1):
        pl.when(nt == n_tiles)(functools.partial(
            _threshold_search, n_tiles, score_ref, hb_ref, thr_ref, room_ref))

    @pl.when(nt * ATTN_KT > TOPK_MAX)
    def _():
        thr = thr_ref[0:1, :]
        room = room_ref[0:1, :]
        krow = lax.broadcasted_iota(jnp.int32, (ATTN_KT, ATTN_KT), 0)
        kcol = lax.broadcasted_iota(jnp.int32, (ATTN_KT, ATTN_KT), 1)
        incl = (kcol <= krow).astype(jnp.bfloat16)

        def tie_body(c, seen):
            sc = score_ref[c]
            tie = sc == thr
            rank = _dot(incl, jnp.where(tie, 1.0, 0.0).astype(jnp.bfloat16)) + seen
            keep_tie = jnp.where(tie, jnp.where(rank <= room, 0.0, NEG_INF), NEG_INF)
            bias_ref[c] = jnp.where(sc > thr, 0.0, keep_tie)
            return rank[ATTN_KT - 1:ATTN_KT, :]

        _tile_loop(0, nt, tie_body, jnp.zeros((1, ATTN_QB), jnp.float32))

    rep = N_HEADS // N_KV_HEADS
    hpp = ATTN_HEADS_PER_PASS
    width = hpp * ATTN_QB
    n_pass = N_HEADS // hpp
    pass_cols = [slice(ps * width, (ps + 1) * width) for ps in range(n_pass)]
    pass_group = [(ps * hpp) // rep for ps in range(n_pass)]

    qpad_ref[...] = jnp.zeros(qpad_ref.shape, qpad_ref.dtype)
    for h in range(N_HEADS):
        g = h // rep
        qpad_ref[g * HEAD_DIM:(g + 1) * HEAD_DIM, h * ATTN_QB:(h + 1) * ATTN_QB] = (
            qt_ref[0, 0, h * HEAD_DIM:(h + 1) * HEAD_DIM, :])

    def qk_body(c, m_accs):
        kt = kn_ref[0, c]
        bias = jnp.concatenate([bias_ref[c]] * hpp, axis=1)
        out = []
        for ps in range(n_pass):
            s = _dot(kt, qpad_ref[:, pass_cols[ps]]) + bias
            s_ref[c, :, pass_cols[ps]] = s
            out.append(jnp.maximum(m_accs[ps], _fold_sublanes(s, jnp.max)))
        return tuple(out)

    m_accs = _tile_loop(
        0, nt, qk_body,
        tuple(jnp.full((SUBLANES, width), NEG_INF, jnp.float32) for _ in range(n_pass)))
    m_rows = [jnp.max(m, axis=0, keepdims=True) for m in m_accs]

    o_ref[...] = jnp.zeros(o_ref.shape, o_ref.dtype)

    def pv_body(c, l_accs):
        out = []
        for ps in range(n_pass):
            g = pass_group[ps]
            p = jnp.exp2(s_ref[c, :, pass_cols[ps]] - m_rows[ps])
            out.append(l_accs[ps] + _fold_sublanes(p, jnp.sum))
            v_blk = vt_ref[0, c, g * HEAD_DIM:(g + 1) * HEAD_DIM, :]
            o_ref[ps] += _dot(v_blk, p.astype(jnp.bfloat16))
        return tuple(out)

    l_accs = _tile_loop(
        0, nt, pv_body,
        tuple(jnp.zeros((SUBLANES, width), jnp.float32) for _ in range(n_pass)))
    for ps in range(n_pass):
        out = o_ref[ps] / jnp.sum(l_accs[ps], axis=0, keepdims=True)
        for r in range(hpp):
            h = ps * hpp + r
            at_ref[0, 0, h * HEAD_DIM:(h + 1) * HEAD_DIM, :] = (
                out[:, r * ATTN_QB:(r + 1) * ATTN_QB].astype(at_ref.dtype))


def _attn(qit, wit, qt, kin, kn, vt):
    b, nq = qt.shape[0], qt.shape[1]
    nt = kn.shape[1]
    per_q = lambda rows: pl.BlockSpec((1, 1, rows, ATTN_QB), lambda i, j: (i, j, 0, 0))
    per_b = lambda arr: pl.BlockSpec((1,) + arr.shape[1:], lambda i, j: (i, 0, 0, 0))
    tile = (nt, ATTN_KT, ATTN_QB)
    return pl.pallas_call(
        _attn_kernel,
        grid=(b, nq),
        in_specs=[per_q(IQ_COLS), per_q(SUBLANES), per_q(Q_COLS),
                  per_b(kin), per_b(kn), per_b(vt)],
        out_specs=per_q(ATTN_WIDTH),
        out_shape=jax.ShapeDtypeStruct((b, nq, ATTN_WIDTH, ATTN_QB), jnp.bfloat16),
        scratch_shapes=[
            pltpu.VMEM(tile, jnp.float32),
            pltpu.VMEM(tile, jnp.bfloat16),
            pltpu.VMEM(tile, jnp.float32),
            pltpu.VMEM((nt, ATTN_KT, N_HEADS * ATTN_QB), jnp.float32),
            pltpu.VMEM((SUBLANES, ATTN_QB), jnp.float32),
            pltpu.VMEM((SUBLANES, ATTN_QB), jnp.float32),
            pltpu.VMEM((N_KV_HEADS * HEAD_DIM, N_HEADS * ATTN_QB), jnp.bfloat16),
            pltpu.VMEM((N_HEADS // ATTN_HEADS_PER_PASS, HEAD_DIM,
                        ATTN_HEADS_PER_PASS * ATTN_QB), jnp.float32),
        ],
        compiler_params=pltpu.CompilerParams(
            dimension_semantics=("parallel", "arbitrary"),
            vmem_limit_bytes=VMEM_LIMIT),
        name="attn",
    )(qit, wit, qt, kin, kn, vt)


def _mixffn_kernel(alpha, x_ref, at_ref, bo_ref, p_ref, wo_ref, win_ref, wout_ref,
                   wple_ref, wgate_ref, g1_ref, b1_ref, g2_ref, b2_ref, o_ref, acc_ref):
    a_mix = jnp.concatenate([_dot_tn(at_ref[0, qb], wo_ref[0:ATTN_WIDTH, :])
                             for qb in range(at_ref.shape[1])], axis=0)
    mix = a_mix + _dot(bo_ref[0], wo_ref[ATTN_WIDTH:, :])
    x = _layer_norm(alpha * x_ref[0] + mix, g1_ref[...], b1_ref[...])
    xb = x.astype(jnp.bfloat16)
    gate_lin = _dot(xb, wgate_ref[...])
    ple = _dot(p_ref[0].astype(jnp.bfloat16), wple_ref[...]) * jax.nn.sigmoid(gate_lin)
    acc_ref[...] = alpha * x + ple
    for c in range(D_FF // FFN_CHUNK):
        cols = slice(c * FFN_CHUNK, (c + 1) * FFN_CHUNK)
        up_cols = slice(D_FF + c * FFN_CHUNK, D_FF + (c + 1) * FFN_CHUNK)
        gate = _dot(xb, win_ref[:, cols])
        up = _dot(xb, win_ref[:, up_cols])
        h = (jax.nn.silu(gate) * up).astype(jnp.bfloat16)
        acc_ref[...] += _dot(h, wout_ref[cols, :])
    o_ref[0] = _layer_norm(acc_ref[...], g2_ref[...], b2_ref[...])


def _mixffn(layer, x, at, bo, p, wo, win, wout, wple, wgate, g1, b1, g2, b2, alpha):
    b, s, d = x.shape
    tm = FFN_TM
    resident = lambda w: _layer_spec(w, layer, pipeline_mode=pl.Buffered(1))
    rows = lambda width: pl.BlockSpec((1, tm, width), lambda i, j: (i, j, 0))
    return pl.pallas_call(
        functools.partial(_mixffn_kernel, alpha),
        grid=(b, s // tm),
        in_specs=[
            rows(d),
            pl.BlockSpec((1, tm // ATTN_QB, ATTN_WIDTH, ATTN_QB), lambda i, j: (i, j, 0, 0)),
            rows(SGU_WIDTH),
            pl.BlockSpec((None, 1, tm, PLE_DIM), lambda i, j: (layer, i, j, 0)),
            resident(wo), resident(win), resident(wout), resident(wple), resident(wgate),
            _layer_spec(g1, layer), _layer_spec(b1, layer),
            _layer_spec(g2, layer), _layer_spec(b2, layer),
        ],
        out_specs=rows(d),
        out_shape=jax.ShapeDtypeStruct(x.shape, x.dtype),
        scratch_shapes=[pltpu.VMEM((tm, d), jnp.float32)],
        compiler_params=pltpu.CompilerParams(
            dimension_semantics=("parallel", "parallel"),
            vmem_limit_bytes=VMEM_LIMIT),
        name="mixffn",
    )(x, at, bo, p, wo, win, wout, wple, wgate, g1, b1, g2, b2)


def kernel(x, p, positions, w_in, w_o, ln1_g, ln1_b, ln2_g, ln2_b, sgu_w, sgu_b,
           sgu_ln_g, sgu_ln_b, w_ffn_in, w_ffn_out, w_ple, w_ple_gate):
    depth = w_in.shape[0]
    alpha = (2 * depth) ** 0.25
    bf = jnp.bfloat16
    attn_cols = ROW_WI + N_IDX_HEADS

    cos_t, sin_t = _rope_tables(positions)

    w_in_bf = w_in.astype(bf)
    wta = jnp.pad(jnp.swapaxes(w_in_bf[:, :, :attn_cols], 1, 2),
                  ((0, 0), (0, ATTN_ROWS - attn_cols), (0, 0)))
    wsgu = w_in_bf[:, :, attn_cols:]
    sgu_bias = jnp.repeat(jnp.swapaxes(sgu_b, 1, 2), SGU_GROUP_DIM, axis=2)
    row = lambda v: v[:, None, :]
    wo, win, wout = w_o.astype(bf), w_ffn_in.astype(bf), w_ffn_out.astype(bf)
    wple, wgate = w_ple.astype(bf), w_ple_gate.astype(bf)

    for i in range(depth):
        qt, kn, vt, qit, kin, wit, bo = _proj(
            i, x, wta, wsgu, cos_t, sin_t, sgu_w, sgu_bias, row(sgu_ln_g), row(sgu_ln_b))
        at = _attn(qit, wit, qt, kin, kn, vt)
        x = _mixffn(i, x, at, bo, p, wo, win, wout, wple, wgate,
                    row(ln1_g), row(ln1_b), row(ln2_g), row(ln2_b), alpha)
    return x
```

```python
import functools
import math

import jax
import jax.numpy as jnp
from jax import lax
from jax.experimental import pallas as pl
from jax.experimental.pallas import tpu as pltpu

D_MODEL = 1024
CHUNK = 64
N_HEADS = 8
N_KV_HEADS = 2
HEAD_DIM = 64
ATTN_WIDTH = N_HEADS * HEAD_DIM
ROPE_DIM = HEAD_DIM // 4
ROPE_HALF = ROPE_DIM // 2
ROPE_THETA = 500000.0
N_IDX_HEADS = 4
IDX_DIM = 64
TOPK_MAX = 256
SGU_CHUNK = 128
SGU_GROUPS = 8
SGU_WIDTH = D_MODEL - ATTN_WIDTH
SGU_GROUP_DIM = SGU_WIDTH // SGU_GROUPS
D_FF = 2816
PLE_DIM = 256
LN_EPS = 1e-5

Q_COLS = N_HEADS * HEAD_DIM
KV_COLS = N_KV_HEADS * HEAD_DIM
IQ_COLS = N_IDX_HEADS * IDX_DIM

ROW_Q = 0
ROW_K = ROW_Q + Q_COLS
ROW_V = ROW_K + KV_COLS
ROW_QI = ROW_V + KV_COLS
ROW_KI = ROW_QI + IQ_COLS
ROW_WI = ROW_KI + IDX_DIM
ATTN_ROWS = 1152

LANES = 128
SUBLANES = 8
BF16_ROWS = 16
VMEM_LIMIT = 56 * 1024 * 1024

PROJ_TM = 1024
FFN_TM = 1024
FFN_CHUNK = 256
ATTN_QB = 256
ATTN_KT = 256
ATTN_HEADS_PER_PASS = 2
TILE_UNROLL = (4, 2, 1)
HI_BITS = 16
REFINE_BITS = 17
KEY_OF_NEG_INF = -0x7F800001
HI_KEY_OF_NEG_INF = -0x7F810000
MIN_NORMAL_KEY = 0x00800000
INT_MIN = -0x80000000
NEG_INF = float("-inf")
LOG2E = math.log2(math.e)


def _layer_norm(y, g, b):
    mu = jnp.mean(y, axis=-1, keepdims=True)
    d = y - mu
    var = jnp.mean(d * d, axis=-1, keepdims=True)
    return d * lax.rsqrt(var + LN_EPS) * g + b


def _dot(a, b):
    return jnp.dot(a, b, preferred_element_type=jnp.float32)


def _dot_nt(a, b):
    return lax.dot_general(a, b, (((1,), (1,)), ((), ())),
                           preferred_element_type=jnp.float32)


def _dot_tn(a, b):
    return lax.dot_general(a, b, (((0,), (0,)), ((), ())),
                           preferred_element_type=jnp.float32)


def _rope_table_kernel(pos_ref, cos_ref, sin_ref):
    seq = pos_ref.shape[-1]
    pos = pos_ref[0].astype(jnp.float32)
    i = lax.broadcasted_iota(jnp.int32, (ROPE_HALF, seq), 0).astype(jnp.float32)
    inv = jnp.power(jnp.float32(ROPE_THETA), i * (-2.0 / ROPE_DIM))
    ang = pos * inv
    cos_ref[0] = jnp.cos(ang)
    sin_ref[0] = jnp.sin(ang)


def _rope_tables(positions):
    b, s = positions.shape
    pos3 = positions.reshape(b, 1, s)
    out = jax.ShapeDtypeStruct((b, ROPE_HALF, s), jnp.float32)
    return pl.pallas_call(
        _rope_table_kernel,
        grid=(b,),
        in_specs=[pl.BlockSpec((1, 1, s), lambda i: (i, 0, 0))],
        out_specs=[pl.BlockSpec((1, ROPE_HALF, s), lambda i: (i, 0, 0))] * 2,
        out_shape=[out, out],
        name="rope_tables",
    )(pos3)


def _rope_rows(ht, cos, sin):
    x1 = ht[0:ROPE_HALF]
    x2 = ht[ROPE_HALF:ROPE_DIM]
    return jnp.concatenate(
        [x1 * cos - x2 * sin, x2 * cos + x1 * sin, ht[ROPE_DIM:]], axis=0)


def _proj_kernel(x_ref, wta_ref, wsgu_ref, cos_ref, sin_ref, sw_ref, sb_ref,
                 lng_ref, lnb_ref,
                 qt_ref, kn_ref, vt_ref, qit_ref, kin_ref, wit_ref, bo_ref):
    tm = x_ref.shape[1]
    xb = x_ref[0].astype(jnp.bfloat16)
    cos = cos_ref[0]
    sin = sin_ref[0]

    def store_query_blocks(ref, rows, val):
        for qb in range(tm // ATTN_QB):
            ref[0, qb, rows, :] = val[:, qb * ATTN_QB:(qb + 1) * ATTN_QB].astype(ref.dtype)

    q_scale = HEAD_DIM ** -0.5 * LOG2E
    qt = _dot_nt(wta_ref[ROW_Q:ROW_K, :], xb)
    for h in range(N_HEADS):
        head = _rope_rows(qt[h * HEAD_DIM:(h + 1) * HEAD_DIM], cos, sin)
        store_query_blocks(qt_ref, slice(h * HEAD_DIM, (h + 1) * HEAD_DIM), head * q_scale)

    kt = _dot_nt(wta_ref[ROW_K:ROW_V, :], xb)
    kt = jnp.concatenate(
        [_rope_rows(kt[g * HEAD_DIM:(g + 1) * HEAD_DIM], cos, sin)
         for g in range(N_KV_HEADS)], axis=0)
    kn = kt.T

    vt = _dot_nt(wta_ref[ROW_V:ROW_QI, :], xb)

    qit = _dot_nt(wta_ref[ROW_QI:ROW_KI, :], xb)
    for h in range(N_IDX_HEADS):
        head = _rope_rows(qit[h * IDX_DIM:(h + 1) * IDX_DIM], cos, sin)
        store_query_blocks(qit_ref, slice(h * IDX_DIM, (h + 1) * IDX_DIM), head)

    kwt = _dot_nt(wta_ref[ROW_KI:ATTN_ROWS, :], xb)
    kit = _rope_rows(kwt[0:IDX_DIM], cos, sin)
    kit = jnp.concatenate([kit, jnp.zeros_like(kit)], axis=0)
    kin = kit.T
    store_query_blocks(wit_ref, slice(0, SUBLANES),
                       kwt[IDX_DIM:IDX_DIM + SUBLANES] * ((N_IDX_HEADS * IDX_DIM) ** -0.5))

    for c in range(tm // ATTN_KT):
        rows = slice(c * ATTN_KT, (c + 1) * ATTN_KT)
        kn_ref[0, c] = kn[rows].astype(kn_ref.dtype)
        kin_ref[0, c] = kin[rows].astype(kin_ref.dtype)
        vt_ref[0, c] = vt[:, rows].astype(vt_ref.dtype)

    uv = _dot(xb, wsgu_ref[...])
    gu = jax.nn.gelu(uv[:, :SGU_WIDTH])
    gv = jax.nn.gelu(uv[:, SGU_WIDTH:])
    gv = _layer_norm(gv, lng_ref[...], lnb_ref[...]).astype(jnp.bfloat16)

    row = lax.broadcasted_iota(jnp.int32, (SGU_CHUNK, SGU_CHUNK), 0)
    col = lax.broadcasted_iota(jnp.int32, (SGU_CHUNK, SGU_CHUNK), 1)
    tri = col <= row
    ws = [jnp.where(tri, sw_ref[g], 0.0).astype(jnp.bfloat16)
          for g in range(SGU_GROUPS)]
    lane = lax.broadcasted_iota(jnp.int32, (SGU_CHUNK, LANES), 1)
    first_group = lane < SGU_GROUP_DIM
    for c in range(tm // SGU_CHUNK):
        rows = slice(c * SGU_CHUNK, (c + 1) * SGU_CHUNK)
        for pr in range(SGU_WIDTH // LANES):
            cols = slice(pr * LANES, (pr + 1) * LANES)
            vblk = gv[rows, cols]
            mixed = jnp.where(first_group,
                              _dot(ws[2 * pr], vblk), _dot(ws[2 * pr + 1], vblk))
            mixed = mixed + sb_ref[:, cols]
            bo_ref[0, rows, cols] = (gu[rows, cols] * mixed).astype(bo_ref.dtype)


def _layer_spec(stacked, layer, **kwargs):
    zeros = (0,) * (stacked.ndim - 1)
    return pl.BlockSpec((None,) + stacked.shape[1:], lambda *_: (layer,) + zeros, **kwargs)


def _proj(layer, x, wta, wsgu, cos_t, sin_t, sgu_w, sgu_bias, ln_g, ln_b):
    b, s, d = x.shape
    tm = PROJ_TM
    nt = s // ATTN_KT
    tpt = tm // ATTN_KT
    grid = (b, s // tm)
    in_specs = [
        pl.BlockSpec((1, tm, d), lambda i, j: (i, j, 0)),
        _layer_spec(wta, layer),
        _layer_spec(wsgu, layer),
        pl.BlockSpec((1, ROPE_HALF, tm), lambda i, j: (i, 0, j)),
        pl.BlockSpec((1, ROPE_HALF, tm), lambda i, j: (i, 0, j)),
        _layer_spec(sgu_w, layer),
        _layer_spec(sgu_bias, layer),
        _layer_spec(ln_g, layer),
        _layer_spec(ln_b, layer),
    ]
    key_rows = pl.BlockSpec((1, tpt, ATTN_KT, LANES), lambda i, j: (i, j, 0, 0))
    key_cols = pl.BlockSpec((1, tpt, LANES, ATTN_KT), lambda i, j: (i, j, 0, 0))
    query_blocks = lambda rows: pl.BlockSpec((1, tm // ATTN_QB, rows, ATTN_QB),
                                             lambda i, j: (i, j, 0, 0))
    out_specs = [
        query_blocks(Q_COLS),
        key_rows,
        key_cols,
        query_blocks(IQ_COLS),
        key_rows,
        query_blocks(SUBLANES),
        pl.BlockSpec((1, tm, SGU_WIDTH), lambda i, j: (i, j, 0)),
    ]
    bf = jnp.bfloat16
    nq = s // ATTN_QB
    out_shape = [
        jax.ShapeDtypeStruct((b, nq, Q_COLS, ATTN_QB), bf),
        jax.ShapeDtypeStruct((b, nt, ATTN_KT, LANES), bf),
        jax.ShapeDtypeStruct((b, nt, LANES, ATTN_KT), bf),
        jax.ShapeDtypeStruct((b, nq, IQ_COLS, ATTN_QB), bf),
        jax.ShapeDtypeStruct((b, nt, ATTN_KT, LANES), bf),
        jax.ShapeDtypeStruct((b, nq, SUBLANES, ATTN_QB), jnp.float32),
        jax.ShapeDtypeStruct((b, s, SGU_WIDTH), bf),
    ]
    return pl.pallas_call(
        _proj_kernel,
        grid=grid,
        in_specs=in_specs,
        out_specs=out_specs,
        out_shape=out_shape,
        compiler_params=pltpu.CompilerParams(
            dimension_semantics=("parallel", "parallel"),
            vmem_limit_bytes=VMEM_LIMIT),
        name="proj",
    )(x, wta, wsgu, cos_t, sin_t, sgu_w, sgu_bias, ln_g, ln_b)


def _fold_sublanes(x, op):
    return op(x.reshape(x.shape[0] // SUBLANES, SUBLANES, x.shape[1]), axis=0)


def _tile_loop(lo, hi, body, init):
    carry, start = init, lo
    for width in TILE_UNROLL:
        def group(i, c, width=width, start=start):
            for u in range(width):
                c = body(start + i * width + u, c)
            return c
        n_groups = (hi - start) // width
        carry = lax.fori_loop(0, n_groups, group, carry)
        start = start + n_groups * width
    return carry


def _count_bf16(ref, n_tiles, pred):
    one = jnp.ones((), jnp.bfloat16)
    zero = jnp.zeros((), jnp.bfloat16)
    groups = ATTN_KT // BF16_ROWS
    n_acc = 4
    assert groups * n_tiles <= 256 * n_acc
    accs = [None] * n_acc
    for c in range(n_tiles):
        m3 = jnp.where(pred(ref[c]), one, zero).reshape(groups, BF16_ROWS, ATTN_QB)
        for i in range(groups):
            k = i % n_acc
            accs[k] = m3[i] if accs[k] is None else accs[k] + m3[i]
    acc = (accs[0] + accs[1]) + (accs[2] + accs[3])
    return jnp.sum(acc.astype(jnp.float32), axis=0, keepdims=True)


def _bit_search(n_bits, count_ge, target):
    def body(i, val):
        trial = val | jnp.left_shift(jnp.int32(1), (n_bits - 1) - i)
        return jnp.where(count_ge(trial) >= target, trial, val)
    return lax.fori_loop(0, n_bits, body, jnp.zeros((1, ATTN_QB), jnp.int32))


def _key_to_float(key):
    bits = key ^ ((key >> 31) & jnp.int32(0x7FFFFFFF))
    return pltpu.bitcast(bits, jnp.float32)


def _hi_bits(uhi):
    key = jnp.maximum(jnp.left_shift(uhi, HI_BITS) ^ jnp.int32(INT_MIN), HI_KEY_OF_NEG_INF)
    key = jnp.where(key > 0, jnp.maximum(key, MIN_NORMAL_KEY), key)
    return key ^ ((key >> 31) & jnp.int32(0x7FFF0000))


def _hi_to_bf16(uhi):
    return pltpu.bitcast(_hi_bits(uhi), jnp.float32).astype(jnp.bfloat16)


def _count_f32(ref, n_tiles, pred):
    parts = [_fold_sublanes(jnp.where(pred(ref[c]), 1.0, 0.0), jnp.sum)
             for c in range(n_tiles)]
    return jnp.sum(functools.reduce(lambda a, b: a + b, parts), axis=0, keepdims=True)


def _threshold_search(n_tiles, score_ref, hb_ref, thr_ref, room_ref):
    target = jnp.float32(TOPK_MAX)

    def coarse_count(uhi):
        cand = _hi_to_bf16(uhi)
        return _count_bf16(hb_ref, n_tiles, lambda blk: blk >= cand)

    uhi = _bit_search(HI_BITS, coarse_count, target)
    bits = _hi_bits(uhi)
    base = (bits ^ ((bits >> 31) & jnp.int32(0x7FFFFFFF))) - (1 << (HI_BITS - 1))

    def fine_count(offset):
        cand = _key_to_float(jnp.maximum(base + offset, KEY_OF_NEG_INF))
        return _count_f32(score_ref, n_tiles, lambda blk: blk >= cand)

    offset = _bit_search(REFINE_BITS, fine_count, target)
    thr = _key_to_float(jnp.maximum(base + offset, KEY_OF_NEG_INF + 1))
    room = target - _count_f32(score_ref, n_tiles, lambda blk: blk > thr)
    thr_ref[...] = jnp.broadcast_to(thr, thr_ref.shape)
    room_ref[...] = jnp.broadcast_to(room, room_ref.shape)


def _attn_kernel(qit_ref, wit_ref, qt_ref, kin_ref, kn_ref, vt_ref, at_ref,
                 score_ref, hb_ref, bias_ref, s_ref, thr_ref, room_ref,
                 qpad_ref, o_ref):
    j = pl.program_id(1)
    nt = j + 1
    max_tiles = score_ref.shape[0]

    zeros_i = jnp.zeros((LANES - IDX_DIM, ATTN_QB), jnp.bfloat16)
    qi_pad = [jnp.concatenate([qit_ref[0, 0, h * IDX_DIM:(h + 1) * IDX_DIM, :], zeros_i], axis=0)
              for h in range(N_IDX_HEADS)]
    wit = wit_ref[0, 0]

    def tile_scores(c):
        kt = kin_ref[0, c]
        sc = jnp.zeros((ATTN_KT, ATTN_QB), jnp.float32)
        for h in range(N_IDX_HEADS):
            sc = sc + jnp.maximum(_dot(kt, qi_pad[h]), 0.0) * wit[h:h + 1, :]
        return sc

    def store_scores(c, sc):
        score_ref[c] = sc
        hb_ref[c] = sc.astype(jnp.bfloat16)

    def score_body(c, carry):
        store_scores(c, tile_scores(c))
        return carry

    _tile_loop(0, j, score_body, 0)
    kchunk = lax.broadcasted_iota(jnp.int32, (ATTN_KT, ATTN_QB), 0) // CHUNK
    qchunk = lax.broadcasted_iota(jnp.int32, (ATTN_KT, ATTN_QB), 1) // CHUNK
    store_scores(j, jnp.where(kchunk > qchunk, NEG_INF, tile_scores(j)))

    @pl.when(nt * ATTN_KT <= TOPK_MAX)
    def _():
        def body(c, carry):
            bias_ref[c] = jnp.where(score_ref[c] == NEG_INF, NEG_INF, 0.0)
            return carry
        lax.fori_loop(0, nt, body, 0)

    for n_tiles in range(TOPK_MAX // ATTN_KT + 1, max_tiles + 1):
        pl.when(nt == n_tiles)(functools.partial(
            _threshold_search, n_tiles, score_ref, hb_ref, thr_ref, room_ref))

    @pl.when(nt * ATTN_KT > TOPK_MAX)
    def _():
        thr = thr_ref[0:1, :]
        room = room_ref[0:1, :]
        krow = lax.broadcasted_iota(jnp.int32, (ATTN_KT, ATTN_KT), 0)
        kcol = lax.broadcasted_iota(jnp.int32, (ATTN_KT, ATTN_KT), 1)
        incl = (kcol <= krow).astype(jnp.bfloat16)

        def tie_body(c, seen):
            sc = score_ref[c]
            tie = sc == thr
            rank = _dot(incl, jnp.where(tie, 1.0, 0.0).astype(jnp.bfloat16)) + seen
            keep_tie = jnp.where(tie, jnp.where(rank <= room, 0.0, NEG_INF), NEG_INF)
            bias_ref[c] = jnp.where(sc > thr, 0.0, keep_tie)
            return rank[ATTN_KT - 1:ATTN_KT, :]

        _tile_loop(0, nt, tie_body, jnp.zeros((1, ATTN_QB), jnp.float32))

    rep = N_HEADS // N_KV_HEADS
    hpp = ATTN_HEADS_PER_PASS
    width = hpp * ATTN_QB
    n_pass = N_HEADS // hpp
    pass_cols = [slice(ps * width, (ps + 1) * width) for ps in range(n_pass)]
    pass_group = [(ps * hpp) // rep for ps in range(n_pass)]

    qpad_ref[...] = jnp.zeros(qpad_ref.shape, qpad_ref.dtype)
    for h in range(N_HEADS):
        g = h // rep
        qpad_ref[g * HEAD_DIM:(g + 1) * HEAD_DIM, h * ATTN_QB:(h + 1) * ATTN_QB] = (
            qt_ref[0, 0, h * HEAD_DIM:(h + 1) * HEAD_DIM, :])

    def qk_body(c, m_accs):
        kt = kn_ref[0, c]
        bias = jnp.concatenate([bias_ref[c]] * hpp, axis=1)
        out = []
        for ps in range(n_pass):
            s = _dot(kt, qpad_ref[:, pass_cols[ps]]) + bias
            s_ref[c, ps] = s
            out.append(jnp.maximum(m_accs[ps], _fold_sublanes(s, jnp.max)))
        return tuple(out)

    m_accs = _tile_loop(
        0, nt, qk_body,
        tuple(jnp.full((SUBLANES, width), NEG_INF, jnp.float32) for _ in range(n_pass)))
    m_rows = [jnp.max(m, axis=0, keepdims=True) for m in m_accs]

    o_ref[...] = jnp.zeros(o_ref.shape, o_ref.dtype)

    def pv_body(c, l_accs):
        out = []
        for ps in range(n_pass):
            g = pass_group[ps]
            p = jnp.exp2(s_ref[c, ps] - m_rows[ps])
            out.append(l_accs[ps] + _fold_sublanes(p, jnp.sum))
            v_blk = vt_ref[0, c, g * HEAD_DIM:(g + 1) * HEAD_DIM, :]
            o_ref[ps] += _dot(v_blk, p.astype(jnp.bfloat16))
        return tuple(out)

    l_accs = _tile_loop(
        0, nt, pv_body,
        tuple(jnp.zeros((SUBLANES, width), jnp.float32) for _ in range(n_pass)))
    for ps in range(n_pass):
        out = o_ref[ps] / jnp.sum(l_accs[ps], axis=0, keepdims=True)
        for r in range(hpp):
            h = ps * hpp + r
            at_ref[0, 0, h * HEAD_DIM:(h + 1) * HEAD_DIM, :] = (
                out[:, r * ATTN_QB:(r + 1) * ATTN_QB].astype(at_ref.dtype))


def _attn(qit, wit, qt, kin, kn, vt):
    b, nq = qt.shape[0], qt.shape[1]
    nt = kn.shape[1]
    per_q = lambda rows: pl.BlockSpec((1, 1, rows, ATTN_QB), lambda i, j: (i, j, 0, 0))
    per_b = lambda arr: pl.BlockSpec((1,) + arr.shape[1:], lambda i, j: (i, 0, 0, 0))
    tile = (nt, ATTN_KT, ATTN_QB)
    return pl.pallas_call(
        _attn_kernel,
        grid=(b, nq),
        in_specs=[per_q(IQ_COLS), per_q(SUBLANES), per_q(Q_COLS),
                  per_b(kin), per_b(kn), per_b(vt)],
        out_specs=per_q(ATTN_WIDTH),
        out_shape=jax.ShapeDtypeStruct((b, nq, ATTN_WIDTH, ATTN_QB), jnp.bfloat16),
        scratch_shapes=[
            pltpu.VMEM(tile, jnp.float32),
            pltpu.VMEM(tile, jnp.bfloat16),
            pltpu.VMEM(tile, jnp.float32),
            pltpu.VMEM((nt, N_HEADS // ATTN_HEADS_PER_PASS, ATTN_KT,
                        ATTN_HEADS_PER_PASS * ATTN_QB), jnp.float32),
            pltpu.VMEM((SUBLANES, ATTN_QB), jnp.float32),
            pltpu.VMEM((SUBLANES, ATTN_QB), jnp.float32),
            pltpu.VMEM((N_KV_HEADS * HEAD_DIM, N_HEADS * ATTN_QB), jnp.bfloat16),
            pltpu.VMEM((N_HEADS // ATTN_HEADS_PER_PASS, HEAD_DIM,
                        ATTN_HEADS_PER_PASS * ATTN_QB), jnp.float32),
        ],
        compiler_params=pltpu.CompilerParams(
            dimension_semantics=("parallel", "arbitrary"),
            vmem_limit_bytes=VMEM_LIMIT),
        name="attn",
    )(qit, wit, qt, kin, kn, vt)


def _mixffn_kernel(alpha, x_ref, at_ref, bo_ref, p_ref, wo_ref, win_ref, wout_ref,
                   wple_ref, wgate_ref, g1_ref, b1_ref, g2_ref, b2_ref, o_ref, acc_ref):
    a_mix = jnp.concatenate([_dot_tn(at_ref[0, qb], wo_ref[0:ATTN_WIDTH, :])
                             for qb in range(at_ref.shape[1])], axis=0)
    mix = a_mix + _dot(bo_ref[0], wo_ref[ATTN_WIDTH:, :])
    x = _layer_norm(alpha * x_ref[0] + mix, g1_ref[...], b1_ref[...])
    xb = x.astype(jnp.bfloat16)
    gate_lin = _dot(xb, wgate_ref[...])
    ple = _dot(p_ref[0].astype(jnp.bfloat16), wple_ref[...]) * jax.nn.sigmoid(gate_lin)
    acc_ref[...] = alpha * x + ple
    for c in range(D_FF // FFN_CHUNK):
        cols = slice(c * FFN_CHUNK, (c + 1) * FFN_CHUNK)
        up_cols = slice(D_FF + c * FFN_CHUNK, D_FF + (c + 1) * FFN_CHUNK)
        gate = _dot(xb, win_ref[:, cols])
        up = _dot(xb, win_ref[:, up_cols])
        h = (jax.nn.silu(gate) * up).astype(jnp.bfloat16)
        acc_ref[...] += _dot(h, wout_ref[cols, :])
    o_ref[0] = _layer_norm(acc_ref[...], g2_ref[...], b2_ref[...])


def _mixffn(layer, x, at, bo, p, wo, win, wout, wple, wgate, g1, b1, g2, b2, alpha):
    b, s, d = x.shape
    tm = FFN_TM
    resident = lambda w: _layer_spec(w, layer, pipeline_mode=pl.Buffered(1))
    rows = lambda width: pl.BlockSpec((1, tm, width), lambda i, j: (i, j, 0))
    return pl.pallas_call(
        functools.partial(_mixffn_kernel, alpha),
        grid=(b, s // tm),
        in_specs=[
            rows(d),
            pl.BlockSpec((1, tm // ATTN_QB, ATTN_WIDTH, ATTN_QB), lambda i, j: (i, j, 0, 0)),
            rows(SGU_WIDTH),
            pl.BlockSpec((None, 1, tm, PLE_DIM), lambda i, j: (layer, i, j, 0)),
            resident(wo), resident(win), resident(wout), resident(wple), resident(wgate),
            _layer_spec(g1, layer), _layer_spec(b1, layer),
            _layer_spec(g2, layer), _layer_spec(b2, layer),
        ],
        out_specs=rows(d),
        out_shape=jax.ShapeDtypeStruct(x.shape, x.dtype),
        scratch_shapes=[pltpu.VMEM((tm, d), jnp.float32)],
        compiler_params=pltpu.CompilerParams(
            dimension_semantics=("parallel", "parallel"),
            vmem_limit_bytes=VMEM_LIMIT),
        name="mixffn",
    )(x, at, bo, p, wo, win, wout, wple, wgate, g1, b1, g2, b2)


def kernel(x, p, positions, w_in, w_o, ln1_g, ln1_b, ln2_g, ln2_b, sgu_w, sgu_b,
           sgu_ln_g, sgu_ln_b, w_ffn_in, w_ffn_out, w_ple, w_ple_gate):
    depth = w_in.shape[0]
    alpha = (2 * depth) ** 0.25
    bf = jnp.bfloat16
    attn_cols = ROW_WI + N_IDX_HEADS

    cos_t, sin_t = _rope_tables(positions)

    w_in_bf = w_in.astype(bf)
    wta = jnp.pad(jnp.swapaxes(w_in_bf[:, :, :attn_cols], 1, 2),
                  ((0, 0), (0, ATTN_ROWS - attn_cols), (0, 0)))
    wsgu = w_in_bf[:, :, attn_cols:]
    sgu_bias = jnp.repeat(jnp.swapaxes(sgu_b, 1, 2), SGU_GROUP_DIM, axis=2)
    row = lambda v: v[:, None, :]
    wo, win, wout = w_o.astype(bf), w_ffn_in.astype(bf), w_ffn_out.astype(bf)
    wple, wgate = w_ple.astype(bf), w_ple_gate.astype(bf)

    for i in range(depth):
        qt, kn, vt, qit, kin, wit, bo = _proj(
            i, x, wta, wsgu, cos_t, sin_t, sgu_w, sgu_bias, row(sgu_ln_g), row(sgu_ln_b))
        at = _attn(qit, wit, qt, kin, kn, vt)
        x = _mixffn(i, x, at, bo, p, wo, win, wout, wple, wgate,
                    row(ln1_g), row(ln1_b), row(ln2_g), row(ln2_b), alpha)
    return x
```
